```python
import jax, jax.numpy as jnp
from jax import lax
import numpy as np

D_MODEL = 4096
BATCH = 32
SEQ = 256
DEPTH = 1
DEC_BATCH = 8
DEC_SEQ = 4096
PAST_LEN = 256

GRID_W = 64
N_HEADS_ATTN = 16
HEAD_DIM_ATTN = 128
D_ATTN = N_HEADS_ATTN * HEAD_DIM_ATTN
WIN_ROWS = 8
WIN_COLS = 16
QBLOCK_COLS = 16
KBLOCK_COLS = QBLOCK_COLS + WIN_COLS
N_HEADS_RWKV = 32
HEAD_DIM_RWKV = 64
D_RWKV = N_HEADS_RWKV * HEAD_DIM_RWKV
DECAY_LORA = 96
ICLR_LORA = 96
GATE_LORA = 256
C_RWKV = 3 * D_RWKV + DECAY_LORA + ICLR_LORA + GATE_LORA
D_IN = 3 * D_ATTN + C_RWKV + 2 * D_MODEL
D_FF = 4 * D_MODEL
RMS_EPS = 1e-6
LNX_EPS = 64e-5

kernel_name = "hybrid_natten_rwkv7_diffusion_step"


def rmsnorm(x, w):
    xf = x.astype(jnp.float32)
    y = xf * lax.rsqrt(jnp.mean(xf * xf, axis=-1, keepdims=True) + RMS_EPS)
    return (y * w.astype(jnp.float32)).astype(x.dtype)


def adaln(cond, w_mod, b_mod):
    m = jax.nn.silu(cond) @ w_mod + b_mod
    return jnp.split(m[..., None, :], 6, axis=-1)


def modulate(h, shift, scale):
    return h * (1 + scale) + shift


def project(h, w_in):
    proj = h @ w_in
    return jnp.split(proj, [3 * D_ATTN, 3 * D_ATTN + C_RWKV], axis=-1)


def attn_heads(pa):
    B, L, _ = pa.shape
    q, k, v = jnp.split(pa, 3, axis=-1)
    shp = (B, L, N_HEADS_ATTN, HEAD_DIM_ATTN)
    return q.reshape(shp), k.reshape(shp), v.reshape(shp)


def context_attention(q, k, v):
    B, L, H, dh = q.shape
    s = jnp.einsum('bqhd,bkhd->bhqk', q, k).astype(jnp.float32) * (dh ** -0.5)
    p = jax.nn.softmax(s, axis=-1).astype(v.dtype)
    return jnp.einsum('bhqk,bkhd->bqhd', p, v).reshape(B, L, H * dh)


def neighbourhood_attention(q, k, v, k_ctx, v_ctx, rpb):
    B, N, H, dh = q.shape
    rows = N // GRID_W
    kh = min(WIN_ROWS, rows)
    nb = GRID_W // QBLOCK_COLS
    kbw = KBLOCK_COLS
    scale = dh ** -0.5
    qcol = np.arange(GRID_W).reshape(nb, QBLOCK_COLS)
    kb_start = np.clip(np.arange(nb) * QBLOCK_COLS - WIN_COLS // 2, 0, GRID_W - kbw)
    kcol = kb_start[:, None] + np.arange(kbw)[None, :]
    cs = np.clip(qcol - WIN_COLS // 2, 0, GRID_W - WIN_COLS)
    col_ok = (kcol[:, None, :] >= cs[:, :, None]) & (kcol[:, None, :] < cs[:, :, None] + WIN_COLS)
    mask = np.broadcast_to(col_ok[:, :, None, :], (nb, QBLOCK_COLS, kh, kbw)).reshape(nb, QBLOCK_COLS, kh * kbw)
    dc_idx = np.clip(kcol[:, None, :] - qcol[:, :, None] + WIN_COLS - 1, 0, 2 * WIN_COLS - 2)
    rpb_c = rpb[:, :, dc_idx]
    qg = q.reshape(B, rows, nb, QBLOCK_COLS, H, dh)
    kg = k.reshape(B, rows, GRID_W, H, dh)
    vg = v.reshape(B, rows, GRID_W, H, dh)
    n_loc = kh * kbw

    def one_row(r):
        rs = jnp.clip(r - WIN_ROWS // 2, 0, rows - kh)
        qr = lax.dynamic_index_in_dim(qg, r, axis=1, keepdims=False)
        kband = lax.dynamic_slice_in_dim(kg, rs, kh, axis=1)
        vband = lax.dynamic_slice_in_dim(vg, rs, kh, axis=1)
        kb = kband[:, :, kcol].transpose(0, 2, 1, 3, 4, 5).reshape(B, nb, n_loc, H, dh)
        vb = vband[:, :, kcol].transpose(0, 2, 1, 3, 4, 5).reshape(B, nb, n_loc, H, dh)
        dr_idx = rs + jnp.arange(kh) - r + WIN_ROWS - 1
        bias = rpb_c[:, dr_idx].transpose(0, 2, 3, 1, 4).reshape(H, nb, QBLOCK_COLS, n_loc)
        s_loc = jnp.einsum('bnqhd,bnkhd->bhnqk', qr, kb).astype(jnp.float32) * scale + bias.astype(jnp.float32)
        s_loc = jnp.where(mask, s_loc, -jnp.inf)
        s_ctx = jnp.einsum('bnqhd,bkhd->bhnqk', qr, k_ctx).astype(jnp.float32) * scale
        p = jax.nn.softmax(jnp.concatenate([s_loc, s_ctx], axis=-1), axis=-1).astype(v.dtype)
        return (jnp.einsum('bhnqk,bnkhd->bnqhd', p[..., :n_loc], vb)
                + jnp.einsum('bhnqk,bkhd->bnqhd', p[..., n_loc:], v_ctx))

    o = lax.map(one_row, jnp.arange(rows))
    return jnp.moveaxis(o, 0, 1).reshape(B, N, H * dh)


def token_shift(p, mu):
    zero = jnp.zeros_like(p[:, :1])
    prev = jnp.concatenate([zero, p[:, :-1]], axis=1)
    nxt = jnp.concatenate([p[:, 1:], zero], axis=1)
    return p + mu * (0.5 * (prev + nxt) - p)


def head_groupnorm(y, w, b):
    B, L, H, N = y.shape
    mu = jnp.mean(y, axis=-1, keepdims=True)
    var = jnp.mean(jnp.square(y - mu), axis=-1, keepdims=True)
    yn = (y - mu) * lax.rsqrt(var + LNX_EPS)
    return yn.reshape(B, L, H * N) * w.astype(jnp.float32) + b.astype(jnp.float32)


def rwkv_scan(S0, r, decay, k, v, kk, b, reverse):
    def step(S, inp):
        r_t, w_t, k_t, v_t, kk_t, b_t = inp
        sa = -jnp.einsum('bhvk,bhk->bhv', S, kk_t)
        S = S * w_t[:, :, None, :] + sa[..., None] * b_t[:, :, None, :] + v_t[..., None] * k_t[:, :, None, :]
        return S, jnp.einsum('bhvk,bhk->bhv', S, r_t)
    xs = tuple(jnp.moveaxis(t, 1, 0) for t in (r, decay, k, v, kk, b))
    S, ys = lax.scan(step, S0, xs, reverse=reverse)
    return S, jnp.moveaxis(ys, 0, 1)


def rwkv_branch(pr, S_f0, S_b0, mu, w0, w2, a0, a2, g2, k_k, k_a, r_k, lnx_w, lnx_b):
    B, L, _ = pr.shape
    f32 = jnp.float32
    xr = token_shift(pr, mu)
    cuts = [D_RWKV, 2 * D_RWKV, 3 * D_RWKV, 3 * D_RWKV + DECAY_LORA, 3 * D_RWKV + DECAY_LORA + ICLR_LORA]
    r, k, v, xw, xa, xg = jnp.split(xr, cuts, axis=-1)

    def heads(t):
        return t.reshape(B, L, N_HEADS_RWKV, HEAD_DIM_RWKV).astype(f32)

    kkh = heads(k * k_k)
    kkh = kkh / jnp.maximum(jnp.sqrt(jnp.sum(kkh * kkh, axis=-1, keepdims=True)), 1e-12)
    rh, vh = heads(r), heads(v)
    g = jax.nn.sigmoid(xg) @ g2

    def direction(d, S0, reverse):
        wl = (w0[d] + jnp.tanh(xw) @ w2[d]).astype(f32)
        wl = -jax.nn.softplus(-wl) - 0.5
        decay = jnp.exp(-jnp.exp(wl))
        a = jax.nn.sigmoid(a0[d] + xa @ a2[d])
        kdh = heads(k * (1 + (a - 1) * k_a))
        S, y = rwkv_scan(S0.astype(f32), rh, heads(decay), kdh, vh, kkh, kkh * heads(a), reverse)
        bonus = jnp.sum(rh * kdh * r_k.astype(f32), axis=-1, keepdims=True) * vh
        return head_groupnorm(y, lnx_w, lnx_b) + bonus.reshape(B, L, D_RWKV), S

    o_f, S_f = direction(0, S_f0, False)
    o_b, S_b = direction(1, S_b0, True)
    return ((o_f + o_b) * g).astype(pr.dtype), S_f, S_b


def merge(o_attn, o_rwkv, pg, w_o_attn, w_o_rwkv, w_out):
    g_a, g_r = jnp.split(jax.nn.sigmoid(pg), 2, axis=-1)
    return (g_a * (o_attn @ w_o_attn) + g_r * (o_rwkv @ w_o_rwkv)) @ w_out


def squared_relu_mlp(h, w1, w2):
    return jnp.square(jax.nn.relu(h @ w1)) @ w2


def setup_inputs(seed: int = 0) -> dict:
    key = jax.random.key(seed)
    ks = jax.random.split(key, 32)
    f32 = jnp.float32

    def nrm(k, shape, scale):
        return jax.random.normal(k, shape, f32) * scale

    L = DEPTH
    return {
        "x_prompt": nrm(ks[0], (BATCH, SEQ, D_MODEL), 1.0),
        "x_sample": nrm(ks[1], (DEC_BATCH, DEC_SEQ, D_MODEL), 1.0),
        "cache_attn_k": nrm(ks[2], (DEC_BATCH, L, PAST_LEN, N_HEADS_ATTN, HEAD_DIM_ATTN), 1.0),
        "cache_attn_v": nrm(ks[3], (DEC_BATCH, L, PAST_LEN, N_HEADS_ATTN, HEAD_DIM_ATTN), 1.0),
        "state_rwkv_fwd": nrm(ks[4], (DEC_BATCH, L, N_HEADS_RWKV, HEAD_DIM_RWKV, HEAD_DIM_RWKV), 0.3),
        "state_rwkv_bwd": nrm(ks[5], (DEC_BATCH, L, N_HEADS_RWKV, HEAD_DIM_RWKV, HEAD_DIM_RWKV), 0.3),
        "c": nrm(ks[6], (DEC_BATCH, D_MODEL), 1.0),
        "c_ctx": nrm(ks[7], (D_MODEL,), 1.0),
        "w_mod": nrm(ks[8], (L, D_MODEL, 6 * D_MODEL), 0.5 * D_MODEL ** -0.5),
        "b_mod": nrm(ks[9], (L, 6 * D_MODEL), 0.02),
        "norm1_w": 1.0 + nrm(ks[10], (L, D_MODEL), 0.05),
        "norm2_w": 1.0 + nrm(ks[11], (L, D_MODEL), 0.05),
        "norm_f_w": 1.0 + nrm(ks[12], (D_MODEL,), 0.05),
        "w_in": nrm(ks[13], (L, D_MODEL, D_IN), D_MODEL ** -0.5),
        "tshift_mu": jax.random.uniform(ks[14], (L, C_RWKV), f32),
        "attn_rpb": nrm(ks[15], (L, N_HEADS_ATTN, 2 * WIN_ROWS - 1, 2 * WIN_COLS - 1), 0.5),
        "rwkv_w0": -1.0 + nrm(ks[16], (L, 2, D_RWKV), 0.5),
        "rwkv_w2": nrm(ks[17], (L, 2, DECAY_LORA, D_RWKV), 0.1),
        "rwkv_a0": nrm(ks[18], (L, 2, D_RWKV), 0.3),
        "rwkv_a2": nrm(ks[19], (L, 2, ICLR_LORA, D_RWKV), ICLR_LORA ** -0.5),
        "rwkv_g2": nrm(ks[20], (L, GATE_LORA, D_RWKV), GATE_LORA ** -0.5),
        "rwkv_k_k": 1.0 + nrm(ks[21], (L, D_RWKV), 0.1),
        "rwkv_k_a": 1.0 + nrm(ks[22], (L, D_RWKV), 0.1),
        "rwkv_r_k": nrm(ks[23], (L, N_HEADS_RWKV, HEAD_DIM_RWKV), 0.1),
        "lnx_w": 1.0 + nrm(ks[24], (L, D_RWKV), 0.05),
        "lnx_b": nrm(ks[25], (L, D_RWKV), 0.02),
        "w_o_attn": nrm(ks[26], (L, D_ATTN, D_MODEL), D_ATTN ** -0.5),
        "w_o_rwkv": nrm(ks[27], (L, D_RWKV, D_MODEL), D_RWKV ** -0.5),
        "w_out": nrm(ks[28], (L, D_MODEL, D_MODEL), D_MODEL ** -0.5),
        "w_mlp1": nrm(ks[29], (L, D_MODEL, D_FF), D_MODEL ** -0.5),
        "w_mlp2": nrm(ks[30], (L, D_FF, D_MODEL), D_FF ** -0.5),
    }


def reference(x_prompt, x_sample, cache_attn_k, cache_attn_v, state_rwkv_fwd, state_rwkv_bwd, c, c_ctx,
              w_mod, b_mod, norm1_w, norm2_w, norm_f_w, w_in, tshift_mu, attn_rpb,
              rwkv_w0, rwkv_w2, rwkv_a0, rwkv_a2, rwkv_g2, rwkv_k_k, rwkv_k_a, rwkv_r_k, lnx_w, lnx_b,
              w_o_attn, w_o_rwkv, w_out, w_mlp1, w_mlp2):
    xc, xs = x_prompt, x_sample
    new_k, new_v, new_sf, new_sb = [], [], [], []
    B_c = x_prompt.shape[0]
    for l in range(DEPTH):
        rw = (tshift_mu[l], rwkv_w0[l], rwkv_w2[l], rwkv_a0[l], rwkv_a2[l], rwkv_g2[l],
              rwkv_k_k[l], rwkv_k_a[l], rwkv_r_k[l], lnx_w[l], lnx_b[l])
        ow = (w_o_attn[l], w_o_rwkv[l], w_out[l])

        sh1, sc1, gt1, sh2, sc2, gt2 = adaln(c_ctx, w_mod[l], b_mod[l])
        h = modulate(rmsnorm(xc, norm1_w[l]), sh1, sc1)
        pa, pr, pg = project(h, w_in[l])
        q, k, v = attn_heads(pa)
        o_attn = context_attention(q, k, v)
        z = jnp.zeros((B_c, N_HEADS_RWKV, HEAD_DIM_RWKV, HEAD_DIM_RWKV), jnp.float32)
        o_rwkv, s_f, s_b = rwkv_branch(pr, z, z, *rw)
        xc = xc + gt1 * merge(o_attn, o_rwkv, pg, *ow)
        h = modulate(rmsnorm(xc, norm2_w[l]), sh2, sc2)
        xc = xc + gt2 * squared_relu_mlp(h, w_mlp1[l], w_mlp2[l])
        new_k.append(k)
        new_v.append(v)
        new_sf.append(s_f)
        new_sb.append(s_b)

        sh1, sc1, gt1, sh2, sc2, gt2 = adaln(c, w_mod[l], b_mod[l])
        h = modulate(rmsnorm(xs, norm1_w[l]), sh1, sc1)
        pa, pr, pg = project(h, w_in[l])
        q, k, v = attn_heads(pa)
        o_attn = neighbourhood_attention(q, k, v, cache_attn_k[:, l], cache_attn_v[:, l], attn_rpb[l])
        o_rwkv, _, _ = rwkv_branch(pr, state_rwkv_fwd[:, l], state_rwkv_bwd[:, l], *rw)
        xs = xs + gt1 * merge(o_attn, o_rwkv, pg, *ow)
        h = modulate(rmsnorm(xs, norm2_w[l]), sh2, sc2)
        xs = xs + gt2 * squared_relu_mlp(h, w_mlp1[l], w_mlp2[l])

    y_prompt = rmsnorm(xc, norm_f_w)
    y_sample = rmsnorm(xs, norm_f_w)
    return (y_prompt, y_sample, jnp.stack(new_k, axis=1), jnp.stack(new_v, axis=1),
            jnp.stack(new_sf, axis=1), jnp.stack(new_sb, axis=1))
```

```python
import functools

import numpy as np
import jax
import jax.numpy as jnp
from jax import lax
from jax.experimental import pallas as pl
from jax.experimental.pallas import tpu as pltpu

GRID_W = 64
WIN_ROWS = 8
WIN_COLS = 16
RMS_EPS = 1e-6
LNX_EPS = 64e-5

LANES = 128
SUBLANES = 8
VMEM_LIMIT_BYTES = 56 * 1024 * 1024

F32 = jnp.float32
BF16 = jnp.bfloat16
NEG_BIG = -1e30


def _cparams(n_axes):
    return pltpu.CompilerParams(dimension_semantics=("arbitrary",) * n_axes,
                                vmem_limit_bytes=VMEM_LIMIT_BYTES)


def _tile(n, pref, unit=LANES):
    if n <= pref:
        return n
    t = (pref // unit) * unit
    while t > unit and n % t:
        t -= unit
    assert n % t == 0, (n, pref, unit)
    return t


def _round_up(n, m):
    return (n + m - 1) // m * m


def _sigmoid(x):
    return 1.0 / (1.0 + jnp.exp(-x))


def _softplus(x):
    return jnp.maximum(x, 0.0) + jnp.log(1.0 + jnp.exp(-jnp.abs(x)))


def _mod_kernel(c_ref, w_ref, b_ref, o_ref):
    c = c_ref[...]
    s = (c * _sigmoid(c)).astype(BF16)
    o_ref[...] = jnp.dot(s, w_ref[...].astype(BF16), preferred_element_type=F32) + b_ref[...]


def _adaln(cond, w_mod, b_mod):
    R, D = cond.shape
    N = w_mod.shape[1]
    tn = _tile(N, 512)
    return pl.pallas_call(
        _mod_kernel,
        grid=(N // tn,),
        in_specs=[pl.BlockSpec((R, D), lambda j: (0, 0)),
                  pl.BlockSpec((D, tn), lambda j: (0, j)),
                  pl.BlockSpec((1, tn), lambda j: (0, j))],
        out_specs=pl.BlockSpec((R, tn), lambda j: (0, j)),
        out_shape=jax.ShapeDtypeStruct((R, N), F32),
        compiler_params=_cparams(1),
        name="adaln",
    )(cond, w_mod, b_mod.reshape(1, N))


def _norm_mod_kernel(x_ref, w_ref, sh_ref, sc_ref, o_ref):
    x = x_ref[...]
    y = x * lax.rsqrt(jnp.mean(x * x, axis=-1, keepdims=True) + RMS_EPS) * w_ref[...]
    o_ref[...] = (y * (1.0 + sc_ref[...]) + sh_ref[...]).astype(o_ref.dtype)


def _norm_kernel(x_ref, w_ref, o_ref):
    x = x_ref[...]
    y = x * lax.rsqrt(jnp.mean(x * x, axis=-1, keepdims=True) + RMS_EPS) * w_ref[...]
    o_ref[...] = y.astype(o_ref.dtype)


def _mod_spec(part, tm, rows_per_cond, cond_base, D):
    return pl.BlockSpec((None, 1, D),
                        lambda i, *_: ((cond_base + (i * tm) // rows_per_cond) * 6 + part, 0, 0))


def _norm_modulate(x, w, mod3, parts, rows_per_cond, cond_base):
    T, D = x.shape
    tm = _tile(min(T, rows_per_cond), 256, SUBLANES)
    return pl.pallas_call(
        _norm_mod_kernel,
        grid=(T // tm,),
        in_specs=[pl.BlockSpec((tm, D), lambda i: (i, 0)),
                  pl.BlockSpec((1, D), lambda i: (0, 0)),
                  _mod_spec(parts[0], tm, rows_per_cond, cond_base, D),
                  _mod_spec(parts[1], tm, rows_per_cond, cond_base, D)],
        out_specs=pl.BlockSpec((tm, D), lambda i: (i, 0)),
        out_shape=jax.ShapeDtypeStruct((T, D), BF16),
        compiler_params=_cparams(1),
        name="norm_modulate",
    )(x, w.reshape(1, D), mod3, mod3)


def _final_norm(x, w):
    T, D = x.shape
    tm = _tile(T, 256, SUBLANES)
    return pl.pallas_call(
        _norm_kernel,
        grid=(T // tm,),
        in_specs=[pl.BlockSpec((tm, D), lambda i: (i, 0)),
                  pl.BlockSpec((1, D), lambda i: (0, 0))],
        out_specs=pl.BlockSpec((tm, D), lambda i: (i, 0)),
        out_shape=jax.ShapeDtypeStruct((T, D), F32),
        compiler_params=_cparams(1),
        name="final_norm",
    )(x, w.reshape(1, D))


def _mm_kernel(x_ref, w_ref, o_ref):
    o_ref[...] = jnp.dot(x_ref[...], w_ref[...], preferred_element_type=F32).astype(o_ref.dtype)


def _mm_relu2_kernel(x_ref, w_ref, o_ref):
    a = jnp.maximum(jnp.dot(x_ref[...], w_ref[...], preferred_element_type=F32), 0.0)
    o_ref[...] = (a * a).astype(o_ref.dtype)


def _matmul(x, w, out_dtype, *, relu2=False, tm_pref=1024, tn_pref=1024, name="matmul"):
    M, K = x.shape
    N = w.shape[1]
    tm, tn = _tile(M, tm_pref, SUBLANES), _tile(N, tn_pref)
    return pl.pallas_call(
        _mm_relu2_kernel if relu2 else _mm_kernel,
        grid=(M // tm, N // tn),
        in_specs=[pl.BlockSpec((tm, K), lambda i, j: (i, 0)),
                  pl.BlockSpec((K, tn), lambda i, j: (0, j))],
        out_specs=pl.BlockSpec((tm, tn), lambda i, j: (i, j)),
        out_shape=jax.ShapeDtypeStruct((M, N), out_dtype),
        compiler_params=_cparams(2),
        name=name,
    )(x, w)


def _merge_kernel(oa_ref, wa_ref, or_ref, wr_ref, ga_ref, gr_ref, o_ref):
    ya = jnp.dot(oa_ref[...], wa_ref[...], preferred_element_type=F32)
    yr = jnp.dot(or_ref[...], wr_ref[...], preferred_element_type=F32)
    o_ref[...] = (_sigmoid(ga_ref[...]) * ya + _sigmoid(gr_ref[...]) * yr).astype(o_ref.dtype)


def _merge(o_attn, w_oa, o_rwkv, w_or, pg):
    T, DA = o_attn.shape
    DR = o_rwkv.shape[1]
    D = w_oa.shape[1]
    tm, tn = _tile(T, 512, SUBLANES), _tile(D, 1024)
    nj = D // tn
    return pl.pallas_call(
        _merge_kernel,
        grid=(T // tm, nj),
        in_specs=[pl.BlockSpec((tm, DA), lambda i, j: (i, 0)),
                  pl.BlockSpec((DA, tn), lambda i, j: (0, j)),
                  pl.BlockSpec((tm, DR), lambda i, j: (i, 0)),
                  pl.BlockSpec((DR, tn), lambda i, j: (0, j)),
                  pl.BlockSpec((tm, tn), lambda i, j: (i, j)),
                  pl.BlockSpec((tm, tn), lambda i, j: (i, j + nj))],
        out_specs=pl.BlockSpec((tm, tn), lambda i, j: (i, j)),
        out_shape=jax.ShapeDtypeStruct((T, D), BF16),
        compiler_params=_cparams(2),
        name="merge",
    )(o_attn, w_oa, o_rwkv, w_or, pg, pg)


def _mm_resid_kernel(x_ref, w_ref, res_ref, gate_ref, o_ref, acc_ref, *, nk):
    k = pl.program_id(2)
    part = jnp.dot(x_ref[...], w_ref[...], preferred_element_type=F32)

    @pl.when(k == 0)
    def _():
        acc_ref[...] = part

    @pl.when(k > 0)
    def _():
        acc_ref[...] += part

    @pl.when(k == nk - 1)
    def _():
        o_ref[...] = res_ref[...] + gate_ref[...] * acc_ref[...]


def _matmul_gated_residual(x, w, res, mod3, part, rows_per_cond, cond_base, name):
    M, K = x.shape
    N = w.shape[1]
    tm = _tile(min(M, rows_per_cond), 1024, SUBLANES)
    tn, tk = _tile(N, 1024), _tile(K, 2048)
    nk = K // tk
    gate_spec = pl.BlockSpec(
        (None, 1, tn), lambda i, j, k: ((cond_base + (i * tm) // rows_per_cond) * 6 + part, 0, j))
    return pl.pallas_call(
        functools.partial(_mm_resid_kernel, nk=nk),
        grid=(M // tm, N // tn, nk),
        in_specs=[pl.BlockSpec((tm, tk), lambda i, j, k: (i, k)),
                  pl.BlockSpec((tk, tn), lambda i, j, k: (k, j)),
                  pl.BlockSpec((tm, tn), lambda i, j, k: (i, j)),
                  gate_spec],
        out_specs=pl.BlockSpec((tm, tn), lambda i, j, k: (i, j)),
        out_shape=jax.ShapeDtypeStruct((M, N), F32),
        scratch_shapes=[pltpu.VMEM((tm, tn), F32)],
        compiler_params=_cparams(3),
        name=name,
    )(x, w, res, mod3)


def _ctx_attn_kernel(q_ref, k_ref, v_ref, o_ref, *, scale):
    q = q_ref[...].astype(BF16)
    k = k_ref[...].astype(BF16)
    s = lax.dot_general(q, k, (((1,), (1,)), ((), ())), preferred_element_type=F32) * scale
    m = jnp.max(s, axis=-1, keepdims=True)
    e = jnp.exp(s - m)
    p = (e / jnp.sum(e, axis=-1, keepdims=True)).astype(BF16)
    o_ref[...] = jnp.dot(p, v_ref[...].astype(BF16), preferred_element_type=F32).astype(o_ref.dtype)


def _context_attention(pa, B, L, H, dh):
    spec = lambda off: pl.BlockSpec((L, dh), lambda b, h: (b, off + h))
    return pl.pallas_call(
        functools.partial(_ctx_attn_kernel, scale=dh ** -0.5),
        grid=(B, H),
        in_specs=[spec(0), spec(H), spec(2 * H)],
        out_specs=pl.BlockSpec((L, dh), lambda b, h: (b, h)),
        out_shape=jax.ShapeDtypeStruct((B * L, H * dh), BF16),
        compiler_params=_cparams(2),
        name="context_attention",
    )(pa, pa, pa)


def _nbr_bias_table(rpb, rows):
    kh = min(WIN_ROWS, rows)
    qc = np.arange(GRID_W)[:, None]
    kc = np.arange(GRID_W)[None, :]
    cs = np.clip(qc - WIN_COLS // 2, 0, GRID_W - WIN_COLS)
    col_ok = (kc >= cs) & (kc < cs + WIN_COLS)
    dc_idx = np.clip(kc - qc + WIN_COLS - 1, 0, 2 * WIN_COLS - 2)
    n_pat = 2 * WIN_ROWS - kh
    dr_idx = np.arange(n_pat)[:, None] + np.arange(kh)[None, :]
    tab = rpb[:, dr_idx][:, :, :, dc_idx]
    tab = jnp.where(col_ok[None, None, None], tab.astype(F32), NEG_BIG)
    tab = tab.transpose(1, 0, 3, 2, 4)
    return tab.reshape(n_pat, rpb.shape[0], GRID_W, kh * GRID_W)


def _nbr_attn_kernel(q_ref, k_ref, v_ref, kc_ref, vc_ref, bias_ref, o_ref, *, rows, kh, scale):
    kc = kc_ref[...].astype(BF16)
    vc = vc_ref[...].astype(BF16)
    nband = kh * GRID_W

    def one_row(r, carry):
        rs = jnp.clip(r - WIN_ROWS // 2, 0, rows - kh)
        pat = rs - r + WIN_ROWS - 1
        q0 = pl.multiple_of(r * GRID_W, GRID_W)
        k0 = pl.multiple_of(rs * GRID_W, GRID_W)
        q = q_ref[pl.ds(q0, GRID_W), :].astype(BF16)
        kb = k_ref[pl.ds(k0, nband), :].astype(BF16)
        vb = v_ref[pl.ds(k0, nband), :].astype(BF16)
        s_loc = lax.dot_general(q, kb, (((1,), (1,)), ((), ())), preferred_element_type=F32) * scale
        s_loc = s_loc + bias_ref[pat]
        s_ctx = lax.dot_general(q, kc, (((1,), (1,)), ((), ())), preferred_element_type=F32) * scale
        m = jnp.maximum(jnp.max(s_loc, axis=-1, keepdims=True), jnp.max(s_ctx, axis=-1, keepdims=True))
        e_loc = jnp.exp(s_loc - m)
        e_ctx = jnp.exp(s_ctx - m)
        den = jnp.sum(e_loc, axis=-1, keepdims=True) + jnp.sum(e_ctx, axis=-1, keepdims=True)
        o = (jnp.dot((e_loc / den).astype(BF16), vb, preferred_element_type=F32)
             + jnp.dot((e_ctx / den).astype(BF16), vc, preferred_element_type=F32))
        o_ref[pl.ds(q0, GRID_W), :] = o.astype(o_ref.dtype)
        return carry

    lax.fori_loop(0, rows, one_row, 0)


def _neighbourhood_attention(pa, k_ctx, v_ctx, bias_tab, B, N, H, dh):
    rows = N // GRID_W
    kh = min(WIN_ROWS, rows)
    Lc = k_ctx.shape[0] // B
    n_pat = bias_tab.shape[0]
    spec = lambda off: pl.BlockSpec((N, dh), lambda b, h: (b, off + h))
    cspec = pl.BlockSpec((Lc, dh), lambda b, h: (b, h))
    return pl.pallas_call(
        functools.partial(_nbr_attn_kernel, rows=rows, kh=kh, scale=dh ** -0.5),
        grid=(B, H),
        in_specs=[spec(0), spec(H), spec(2 * H), cspec, cspec,
                  pl.BlockSpec((n_pat, None, GRID_W, kh * GRID_W), lambda b, h: (0, h, 0, 0))],
        out_specs=pl.BlockSpec((N, dh), lambda b, h: (b, h)),
        out_shape=jax.ShapeDtypeStruct((B * N, H * dh), BF16),
        compiler_params=_cparams(2),
        name="neighbourhood_attention",
    )(pa, pa, pa, k_ctx, v_ctx, bias_tab)


def _rwkv_pre_kernel(p_ref, prev_ref, next_ref, mu_ref, w0_ref, w2_ref, a0_ref, a2_ref, g2_ref,
                     r_ref, k_ref, v_ref, wp_ref, ap_ref, g_ref, *, tiles_per_seq, DR, LW, LA):
    i = pl.program_id(0)
    p = p_ref[...]
    tl = p.shape[0]
    row = lax.broadcasted_iota(jnp.int32, p.shape, 0)
    first = (i % tiles_per_seq) == 0
    last = (i % tiles_per_seq) == tiles_per_seq - 1
    halo_prev = jnp.where(first, 0.0, prev_ref[SUBLANES - 1:SUBLANES, :])
    halo_next = jnp.where(last, 0.0, next_ref[0:1, :])
    prev = jnp.where(row == 0, halo_prev, pltpu.roll(p, 1, 0))
    nxt = jnp.where(row == tl - 1, halo_next, pltpu.roll(p, tl - 1, 0))
    xr = p + mu_ref[...] * (0.5 * (prev + nxt) - p)
    r_ref[...] = xr[:, 0:DR]
    k_ref[...] = xr[:, DR:2 * DR]
    v_ref[...] = xr[:, 2 * DR:3 * DR]
    o = 3 * DR
    tw = jnp.tanh(xr[:, o:o + LW]).astype(BF16)
    xa = xr[:, o + LW:o + LW + LA].astype(BF16)
    sg = _sigmoid(xr[:, o + LW + LA:]).astype(BF16)
    for d in range(2):
        wp_ref[d] = w0_ref[d] + jnp.dot(tw, w2_ref[d], preferred_element_type=F32)
        ap_ref[d] = a0_ref[d] + jnp.dot(xa, a2_ref[d], preferred_element_type=F32)
    g_ref[...] = jnp.dot(sg, g2_ref[...], preferred_element_type=F32)


def _rwkv_pre(pr, L, mu, w0, w2, a0, a2, g2, DR, LW, LA):
    T, W = pr.shape
    tl = _tile(L, 128, SUBLANES)
    tiles_per_seq = L // tl
    hb = tl // SUBLANES
    nhb = T // SUBLANES
    LG = g2.shape[0]
    full = lambda shape: pl.BlockSpec(shape, lambda i: (0,) * len(shape))
    out_main = pl.BlockSpec((tl, DR), lambda i: (i, 0))
    out_dir = pl.BlockSpec((2, tl, DR), lambda i: (0, i, 0))
    return pl.pallas_call(
        functools.partial(_rwkv_pre_kernel, tiles_per_seq=tiles_per_seq, DR=DR, LW=LW, LA=LA),
        grid=(T // tl,),
        in_specs=[pl.BlockSpec((tl, W), lambda i: (i, 0)),
                  pl.BlockSpec((SUBLANES, W), lambda i: (jnp.maximum(i * hb - 1, 0), 0)),
                  pl.BlockSpec((SUBLANES, W), lambda i: (jnp.minimum((i + 1) * hb, nhb - 1), 0)),
                  full((1, W)), full((2, 1, DR)), full((2, LW, DR)), full((2, 1, DR)),
                  full((2, LA, DR)), full((LG, DR))],
        out_specs=[out_main, out_main, out_main, out_dir, out_dir, out_main],
        out_shape=[jax.ShapeDtypeStruct((T, DR), F32)] * 3
                  + [jax.ShapeDtypeStruct((2, T, DR), F32)] * 2
                  + [jax.ShapeDtypeStruct((T, DR), F32)],
        compiler_params=_cparams(1),
        name="rwkv_pre",
    )(pr, pr, pr, mu, w0, w2, a0, a2, g2)


def _scan_kernel(r_ref, k_ref, v_ref, wp_ref, ap_ref, kk_w_ref, ka_w_ref, rk_w_ref, lw_ref, lb_ref,
                 s0_ref, o_ref, sT_ref, S_scr, y_scr, *, tb, nt, n):
    d = pl.program_id(0)
    tblk = pl.program_id(2)

    @pl.when(tblk == 0)
    def _():
        S_scr[...] = s0_ref[...]

    k_k = kk_w_ref[...]
    k_a = ka_w_ref[...]
    r_k = rk_w_ref[...]
    lnw = lw_ref[...]
    lnb = lb_ref[...]

    def step(i, carry):
        t = i + d * (tb - 1 - 2 * i)
        r_t = r_ref[t]
        k_t = k_ref[t]
        v_t = v_ref[t]
        wl = -_softplus(-wp_ref[t]) - 0.5
        decay = jnp.exp(-jnp.exp(wl))
        a = _sigmoid(ap_ref[t])
        kk = k_t * k_k
        kk = kk / jnp.maximum(jnp.sqrt(jnp.sum(kk * kk, axis=0, keepdims=True)), 1e-12)
        kd = k_t * (1.0 + (a - 1.0) * k_a)
        b = kk * a
        for v in range(n):
            Sv = S_scr[v]
            sa = -jnp.sum(Sv * kk, axis=0, keepdims=True)
            Sv = Sv * decay + sa * b + v_t[v:v + 1, :] * kd
            S_scr[v] = Sv
            y_scr[v:v + 1, :] = jnp.sum(Sv * r_t, axis=0, keepdims=True)
        y = y_scr[...]
        mean = jnp.mean(y, axis=0, keepdims=True)
        yc = y - mean
        var = jnp.mean(yc * yc, axis=0, keepdims=True)
        bonus = jnp.sum(r_t * kd * r_k, axis=0, keepdims=True) * v_t
        o_ref[t] = yc * lax.rsqrt(var + LNX_EPS) * lnw + lnb + bonus
        return carry

    lax.fori_loop(0, tb, step, 0)

    @pl.when(tblk == nt - 1)
    def _():
        sT_ref[...] = S_scr[...]


def _rwkv_scan(r, k, v, wp, ap, k_k, k_a, r_k, lnw, lnb, s0):
    L, n, NC = r.shape
    tb = _tile(L, 32, 1)
    nt = L // tb
    tmap = lambda d, t: t + d * (nt - 1 - 2 * t)
    seq = pl.BlockSpec((tb, n, LANES), lambda d, c, t: (tmap(d, t), 0, c))
    seq_d = pl.BlockSpec((None, tb, n, LANES), lambda d, c, t: (d, tmap(d, t), 0, c))
    wgt = pl.BlockSpec((n, LANES), lambda d, c, t: (0, c))
    st = pl.BlockSpec((None, n, n, LANES), lambda d, c, t: (d, 0, 0, c))
    return pl.pallas_call(
        functools.partial(_scan_kernel, tb=tb, nt=nt, n=n),
        grid=(2, NC // LANES, nt),
        in_specs=[seq, seq, seq, seq_d, seq_d, wgt, wgt, wgt, wgt, wgt, st],
        out_specs=[seq_d, st],
        out_shape=[jax.ShapeDtypeStruct((2, L, n, NC), F32),
                   jax.ShapeDtypeStruct((2, n, n, NC), F32)],
        scratch_shapes=[pltpu.VMEM((n, n, LANES), F32), pltpu.VMEM((n, LANES), F32)],
        compiler_params=_cparams(3),
        name="rwkv_scan",
    )(r, k, v, wp, ap, k_k, k_a, r_k, lnw, lnb, s0)


def _gate_kernel(of_ref, ob_ref, g_ref, o_ref):
    o_ref[...] = ((of_ref[...] + ob_ref[...]) * g_ref[...]).astype(o_ref.dtype)


def _rwkv_gate(o_dirs, g):
    _, T, DR = o_dirs.shape
    tm = _tile(T, 512, SUBLANES)
    dspec = lambda d: pl.BlockSpec((None, tm, DR), lambda i: (d, i, 0))
    return pl.pallas_call(
        _gate_kernel,
        grid=(T // tm,),
        in_specs=[dspec(0), dspec(1), pl.BlockSpec((tm, DR), lambda i: (i, 0))],
        out_specs=pl.BlockSpec((tm, DR), lambda i: (i, 0)),
        out_shape=jax.ShapeDtypeStruct((T, DR), BF16),
        compiler_params=_cparams(1),
        name="rwkv_gate",
    )(o_dirs, o_dirs, g)


def _to_chain_layout(x, B, L, H, n, NCp):
    lead = x.shape[:-2]
    nl = len(lead)
    x = x.reshape(lead + (B, L, H, n))
    x = jnp.transpose(x, tuple(range(nl)) + (nl + 1, nl + 3, nl, nl + 2))
    x = x.reshape(lead + (L, n, B * H))
    if NCp != B * H:
        x = jnp.pad(x, [(0, 0)] * (nl + 2) + [(0, NCp - B * H)])
    return x


def _from_chain_layout(x, B, L, H, n):
    x = x[..., :B * H].reshape(2, L, n, B, H)
    return jnp.transpose(x, (0, 3, 1, 4, 2)).reshape(2, B * L, H * n)


def _chain_weight(w, B, H, n, NCp):
    t = jnp.tile(w.reshape(H, n).T, (1, B))
    if NCp != B * H:
        t = jnp.pad(t, [(0, 0), (0, NCp - B * H)])
    return t


def _rwkv_branch(pr, B, L, H, n, s_f0, s_b0, rw):
    DR = H * n
    NC = B * H
    NCp = _round_up(NC, LANES)
    r, k, v, wp, ap, g = _rwkv_pre(pr, L, rw["mu"], rw["w0"], rw["w2"], rw["a0"], rw["a2"], rw["g2"],
                                   DR, rw["LW"], rw["LA"])
    tc = functools.partial(_to_chain_layout, B=B, L=L, H=H, n=n, NCp=NCp)
    cw = functools.partial(_chain_weight, B=B, H=H, n=n, NCp=NCp)
    s0 = jnp.stack([s_f0, s_b0]).astype(F32)
    s0 = jnp.transpose(s0, (0, 3, 4, 1, 2)).reshape(2, n, n, NC)
    if NCp != NC:
        s0 = jnp.pad(s0, [(0, 0)] * 3 + [(0, NCp - NC)])
    o_dirs, sT = _rwkv_scan(tc(r), tc(k), tc(v), tc(wp), tc(ap),
                            cw(rw["k_k"]), cw(rw["k_a"]), cw(rw["r_k"]), cw(rw["lnx_w"]), cw(rw["lnx_b"]), s0)
    o_rwkv = _rwkv_gate(_from_chain_layout(o_dirs, B, L, H, n), g)
    sT = jnp.transpose(sT[..., :NC].reshape(2, n, n, B, H), (0, 3, 4, 1, 2))
    return o_rwkv, sT[0], sT[1]


def _pad_rows(w, rows):
    return jnp.pad(w, [(0, 0)] * (w.ndim - 2) + [(0, rows - w.shape[-2]), (0, 0)])


def kernel(x_prompt, x_sample, cache_attn_k, cache_attn_v, state_rwkv_fwd, state_rwkv_bwd, c, c_ctx,
           w_mod, b_mod, norm1_w, norm2_w, norm_f_w, w_in, tshift_mu, attn_rpb,
           rwkv_w0, rwkv_w2, rwkv_a0, rwkv_a2, rwkv_g2, rwkv_k_k, rwkv_k_a, rwkv_r_k, lnx_w, lnx_b,
           w_o_attn, w_o_rwkv, w_out, w_mlp1, w_mlp2):
    Bc, Lc, D = x_prompt.shape
    Bs, Ls, _ = x_sample.shape
    depth = w_mod.shape[0]
    HA, dh = cache_attn_k.shape[3], cache_attn_k.shape[4]
    HR, n = state_rwkv_fwd.shape[2], state_rwkv_fwd.shape[3]
    DA, DR = HA * dh, HR * n
    lw, la, lg = rwkv_w2.shape[2], rwkv_a2.shape[2], rwkv_g2.shape[1]
    LW, LA, LG = _round_up(lw, LANES), _round_up(la, LANES), _round_up(lg, LANES)
    past = cache_attn_k.shape[2]
    rows = Ls // GRID_W

    xc = x_prompt.reshape(Bc * Lc, D)
    xs = x_sample.reshape(Bs * Ls, D)
    cond_rows = _round_up(Bs + 1, SUBLANES)
    cond = jnp.zeros((cond_rows, D), F32).at[:Bs].set(c).at[Bs].set(c_ctx)
    pad_c = lambda a, wdt: jnp.pad(a, [(0, 0)] * (a.ndim - 1) + [(0, wdt - a.shape[-1])])

    new_k, new_v, new_sf, new_sb = [], [], [], []
    for l in range(depth):
        mod3 = _adaln(cond, w_mod[l], b_mod[l]).reshape(cond_rows * 6, 1, D)

        wi = w_in[l]
        o1, o2 = 3 * DA, 3 * DA + 3 * DR
        w_a = wi[:, :o1].astype(BF16)
        w_r = jnp.concatenate([wi[:, o1:o2], pad_c(wi[:, o2:o2 + lw], LW),
                               pad_c(wi[:, o2 + lw:o2 + lw + la], LA),
                               pad_c(wi[:, o2 + lw + la:o2 + lw + la + lg], LG)], axis=1).astype(BF16)
        w_g = wi[:, o2 + lw + la + lg:].astype(BF16)
        mu = tshift_mu[l]
        mu_r = jnp.concatenate([mu[:3 * DR], pad_c(mu[3 * DR:3 * DR + lw], LW),
                                pad_c(mu[3 * DR + lw:3 * DR + lw + la], LA),
                                pad_c(mu[3 * DR + lw + la:], LG)]).reshape(1, -1)
        rw = dict(mu=mu_r, LW=LW, LA=LA,
                  w0=rwkv_w0[l].reshape(2, 1, DR), w2=_pad_rows(rwkv_w2[l], LW).astype(BF16),
                  a0=rwkv_a0[l].reshape(2, 1, DR), a2=_pad_rows(rwkv_a2[l], LA).astype(BF16),
                  g2=_pad_rows(rwkv_g2[l], LG).astype(BF16),
                  k_k=rwkv_k_k[l], k_a=rwkv_k_a[l], r_k=rwkv_r_k[l].reshape(-1),
                  lnx_w=lnx_w[l], lnx_b=lnx_b[l])
        w_oa, w_or = w_o_attn[l].astype(BF16), w_o_rwkv[l].astype(BF16)
        w_o, w_1, w_2 = w_out[l].astype(BF16), w_mlp1[l].astype(BF16), w_mlp2[l].astype(BF16)
        bias_tab = _nbr_bias_table(attn_rpb[l], rows)

        def block(x, B, L, rows_per_cond, cond_base, attention, s_f0, s_b0):
            h = _norm_modulate(x, norm1_w[l], mod3, (0, 1), rows_per_cond, cond_base)
            pa = _matmul(h, w_a, F32, name="proj_attn")
            pr = _matmul(h, w_r, F32, tn_pref=512, name="proj_rwkv")
            pg = _matmul(h, w_g, F32, name="proj_gate")
            o_attn = attention(pa)
            o_rwkv, s_f, s_b = _rwkv_branch(pr, B, L, HR, n, s_f0, s_b0, rw)
            merged = _merge(o_attn, w_oa, o_rwkv, w_or, pg)
            x = _matmul_gated_residual(merged, w_o, x, mod3, 2, rows_per_cond, cond_base, "out_proj")
            h = _norm_modulate(x, norm2_w[l], mod3, (3, 4), rows_per_cond, cond_base)
            a = _matmul(h, w_1, BF16, relu2=True, name="mlp_up")
            x = _matmul_gated_residual(a, w_2, x, mod3, 5, rows_per_cond, cond_base, "mlp_down")
            return x, pa, s_f, s_b

        z = jnp.zeros((Bc, HR, n, n), F32)
        xc, pa_c, s_f, s_b = block(
            xc, Bc, Lc, Bc * Lc, Bs,
            lambda pa: _context_attention(pa, Bc, Lc, HA, dh), z, z)
        new_k.append(pa_c[:, DA:2 * DA].reshape(Bc, Lc, HA, dh))
        new_v.append(pa_c[:, 2 * DA:].reshape(Bc, Lc, HA, dh))
        new_sf.append(s_f)
        new_sb.append(s_b)

        kc = cache_attn_k[:, l].reshape(Bs * past, DA)
        vc = cache_attn_v[:, l].reshape(Bs * past, DA)
        xs, _, _, _ = block(
            xs, Bs, Ls, Ls, 0,
            lambda pa: _neighbourhood_attention(pa, kc, vc, bias_tab, Bs, Ls, HA, dh),
            state_rwkv_fwd[:, l], state_rwkv_bwd[:, l])

    y_prompt = _final_norm(xc, norm_f_w).reshape(Bc, Lc, D)
    y_sample = _final_norm(xs, norm_f_w).reshape(Bs, Ls, D)
    return (y_prompt, y_sample, jnp.stack(new_k, axis=1), jnp.stack(new_v, axis=1),
            jnp.stack(new_sf, axis=1), jnp.stack(new_sb, axis=1))
```

```python
import functools

import numpy as np
import jax
import jax.numpy as jnp
from jax import lax
from jax.experimental import pallas as pl
from jax.experimental.pallas import tpu as pltpu

GRID_W = 64
WIN_ROWS = 8
WIN_COLS = 16
RMS_EPS = 1e-6
LNX_EPS = 64e-5

LANES = 128
SUBLANES = 8
VMEM_LIMIT_BYTES = 56 * 1024 * 1024

F32 = jnp.float32
BF16 = jnp.bfloat16
NEG_BIG = -1e30


def _cparams(n_axes):
    return pltpu.CompilerParams(dimension_semantics=("arbitrary",) * n_axes,
                                vmem_limit_bytes=VMEM_LIMIT_BYTES)


def _tile(n, pref, unit=LANES):
    if n <= pref:
        return n
    t = (pref // unit) * unit
    while t > unit and n % t:
        t -= unit
    assert n % t == 0, (n, pref, unit)
    return t


def _round_up(n, m):
    return (n + m - 1) // m * m


def _sigmoid(x):
    return 1.0 / (1.0 + jnp.exp(-x))


def _softplus(x):
    return jnp.maximum(x, 0.0) + jnp.log(1.0 + jnp.exp(-jnp.abs(x)))


def _mod_kernel(c_ref, w_ref, b_ref, o_ref):
    c = c_ref[...]
    s = (c * _sigmoid(c)).astype(BF16)
    o_ref[...] = jnp.dot(s, w_ref[...].astype(BF16), preferred_element_type=F32) + b_ref[...]


def _adaln(cond, w_mod, b_mod):
    R, D = cond.shape
    N = w_mod.shape[1]
    tn = _tile(N, 512)
    return pl.pallas_call(
        _mod_kernel,
        grid=(N // tn,),
        in_specs=[pl.BlockSpec((R, D), lambda j: (0, 0)),
                  pl.BlockSpec((D, tn), lambda j: (0, j)),
                  pl.BlockSpec((1, tn), lambda j: (0, j))],
        out_specs=pl.BlockSpec((R, tn), lambda j: (0, j)),
        out_shape=jax.ShapeDtypeStruct((R, N), F32),
        compiler_params=_cparams(1),
        name="adaln",
    )(cond, w_mod, b_mod.reshape(1, N))


def _norm_mod_kernel(x_ref, w_ref, sh_ref, sc_ref, o_ref):
    x = x_ref[...]
    y = x * lax.rsqrt(jnp.mean(x * x, axis=-1, keepdims=True) + RMS_EPS) * w_ref[...]
    o_ref[...] = (y * (1.0 + sc_ref[...]) + sh_ref[...]).astype(o_ref.dtype)


def _norm_kernel(x_ref, w_ref, o_ref):
    x = x_ref[...]
    y = x * lax.rsqrt(jnp.mean(x * x, axis=-1, keepdims=True) + RMS_EPS) * w_ref[...]
    o_ref[...] = y.astype(o_ref.dtype)


def _mod_spec(part, tm, rows_per_cond, cond_base, D):
    return pl.BlockSpec((None, 1, D),
                        lambda i, *_: ((cond_base + (i * tm) // rows_per_cond) * 6 + part, 0, 0))


def _norm_modulate(x, w, mod3, parts, rows_per_cond, cond_base):
    T, D = x.shape
    tm = _tile(min(T, rows_per_cond), 256, SUBLANES)
    return pl.pallas_call(
        _norm_mod_kernel,
        grid=(T // tm,),
        in_specs=[pl.BlockSpec((tm, D), lambda i: (i, 0)),
                  pl.BlockSpec((1, D), lambda i: (0, 0)),
                  _mod_spec(parts[0], tm, rows_per_cond, cond_base, D),
                  _mod_spec(parts[1], tm, rows_per_cond, cond_base, D)],
        out_specs=pl.BlockSpec((tm, D), lambda i: (i, 0)),
        out_shape=jax.ShapeDtypeStruct((T, D), BF16),
        compiler_params=_cparams(1),
        name="norm_modulate",
    )(x, w.reshape(1, D), mod3, mod3)


def _final_norm(x, w):
    T, D = x.shape
    tm = _tile(T, 256, SUBLANES)
    return pl.pallas_call(
        _norm_kernel,
        grid=(T // tm,),
        in_specs=[pl.BlockSpec((tm, D), lambda i: (i, 0)),
                  pl.BlockSpec((1, D), lambda i: (0, 0))],
        out_specs=pl.BlockSpec((tm, D), lambda i: (i, 0)),
        out_shape=jax.ShapeDtypeStruct((T, D), F32),
        compiler_params=_cparams(1),
        name="final_norm",
    )(x, w.reshape(1, D))


def _mm_kernel(x_ref, w_ref, o_ref):
    o_ref[...] = jnp.dot(x_ref[...], w_ref[...], preferred_element_type=F32).astype(o_ref.dtype)


def _mm_relu2_kernel(x_ref, w_ref, o_ref):
    a = jnp.maximum(jnp.dot(x_ref[...], w_ref[...], preferred_element_type=F32), 0.0)
    o_ref[...] = (a * a).astype(o_ref.dtype)


def _matmul(x, w, out_dtype, *, relu2=False, tm_pref=1024, tn_pref=1024, name="matmul"):
    M, K = x.shape
    N = w.shape[1]
    tm, tn = _tile(M, tm_pref, SUBLANES), _tile(N, tn_pref)
    return pl.pallas_call(
        _mm_relu2_kernel if relu2 else _mm_kernel,
        grid=(M // tm, N // tn),
        in_specs=[pl.BlockSpec((tm, K), lambda i, j: (i, 0)),
                  pl.BlockSpec((K, tn), lambda i, j: (0, j))],
        out_specs=pl.BlockSpec((tm, tn), lambda i, j: (i, j)),
        out_shape=jax.ShapeDtypeStruct((M, N), out_dtype),
        compiler_params=_cparams(2),
        name=name,
    )(x, w)


def _merge_kernel(oa_ref, wa_ref, or_ref, wr_ref, ga_ref, gr_ref, o_ref):
    ya = jnp.dot(oa_ref[...], wa_ref[...], preferred_element_type=F32)
    yr = jnp.dot(or_ref[...], wr_ref[...], preferred_element_type=F32)
    o_ref[...] = (_sigmoid(ga_ref[...]) * ya + _sigmoid(gr_ref[...]) * yr).astype(o_ref.dtype)


def _merge(o_attn, w_oa, o_rwkv, w_or, pg):
    T, DA = o_attn.shape
    DR = o_rwkv.shape[1]
    D = w_oa.shape[1]
    tm, tn = _tile(T, 512, SUBLANES), _tile(D, 1024)
    nj = D // tn
    return pl.pallas_call(
        _merge_kernel,
        grid=(T // tm, nj),
        in_specs=[pl.BlockSpec((tm, DA), lambda i, j: (i, 0)),
                  pl.BlockSpec((DA, tn), lambda i, j: (0, j)),
                  pl.BlockSpec((tm, DR), lambda i, j: (i, 0)),
                  pl.BlockSpec((DR, tn), lambda i, j: (0, j)),
                  pl.BlockSpec((tm, tn), lambda i, j: (i, j)),
                  pl.BlockSpec((tm, tn), lambda i, j: (i, j + nj))],
        out_specs=pl.BlockSpec((tm, tn), lambda i, j: (i, j)),
        out_shape=jax.ShapeDtypeStruct((T, D), BF16),
        compiler_params=_cparams(2),
        name="merge",
    )(o_attn, w_oa, o_rwkv, w_or, pg, pg)


def _mm_resid_kernel(x_ref, w_ref, res_ref, gate_ref, o_ref, acc_ref, *, nk):
    k = pl.program_id(2)
    part = jnp.dot(x_ref[...], w_ref[...], preferred_element_type=F32)

    @pl.when(k == 0)
    def _():
        acc_ref[...] = part

    @pl.when(k > 0)
    def _():
        acc_ref[...] += part

    @pl.when(k == nk - 1)
    def _():
        o_ref[...] = res_ref[...] + gate_ref[...] * acc_ref[...]


def _matmul_gated_residual(x, w, res, mod3, part, rows_per_cond, cond_base, name):
    M, K = x.shape
    N = w.shape[1]
    tm = _tile(min(M, rows_per_cond), 1024, SUBLANES)
    tn, tk = _tile(N, 1024), _tile(K, 2048)
    nk = K // tk
    gate_spec = pl.BlockSpec(
        (None, 1, tn), lambda i, j, k: ((cond_base + (i * tm) // rows_per_cond) * 6 + part, 0, j))
    return pl.pallas_call(
        functools.partial(_mm_resid_kernel, nk=nk),
        grid=(M // tm, N // tn, nk),
        in_specs=[pl.BlockSpec((tm, tk), lambda i, j, k: (i, k)),
                  pl.BlockSpec((tk, tn), lambda i, j, k: (k, j)),
                  pl.BlockSpec((tm, tn), lambda i, j, k: (i, j)),
                  gate_spec],
        out_specs=pl.BlockSpec((tm, tn), lambda i, j, k: (i, j)),
        out_shape=jax.ShapeDtypeStruct((M, N), F32),
        scratch_shapes=[pltpu.VMEM((tm, tn), F32)],
        compiler_params=_cparams(3),
        name=name,
    )(x, w, res, mod3)


def _ctx_attn_kernel(q_ref, k_ref, v_ref, o_ref, *, scale):
    q = q_ref[...].astype(BF16)
    k = k_ref[...].astype(BF16)
    s = lax.dot_general(q, k, (((1,), (1,)), ((), ())), preferred_element_type=F32) * scale
    m = jnp.max(s, axis=-1, keepdims=True)
    e = jnp.exp(s - m)
    p = (e / jnp.sum(e, axis=-1, keepdims=True)).astype(BF16)
    o_ref[...] = jnp.dot(p, v_ref[...].astype(BF16), preferred_element_type=F32).astype(o_ref.dtype)


def _context_attention(pa, B, L, H, dh):
    spec = lambda off: pl.BlockSpec((L, dh), lambda b, h: (b, off + h))
    return pl.pallas_call(
        functools.partial(_ctx_attn_kernel, scale=dh ** -0.5),
        grid=(B, H),
        in_specs=[spec(0), spec(H), spec(2 * H)],
        out_specs=pl.BlockSpec((L, dh), lambda b, h: (b, h)),
        out_shape=jax.ShapeDtypeStruct((B * L, H * dh), BF16),
        compiler_params=_cparams(2),
        name="context_attention",
    )(pa, pa, pa)


def _nbr_bias_table(rpb, rows):
    kh = min(WIN_ROWS, rows)
    qc = np.arange(GRID_W)[:, None]
    kc = np.arange(GRID_W)[None, :]
    cs = np.clip(qc - WIN_COLS // 2, 0, GRID_W - WIN_COLS)
    col_ok = (kc >= cs) & (kc < cs + WIN_COLS)
    dc_idx = np.clip(kc - qc + WIN_COLS - 1, 0, 2 * WIN_COLS - 2)
    n_pat = 2 * WIN_ROWS - kh
    dr_idx = np.arange(n_pat)[:, None] + np.arange(kh)[None, :]
    tab = rpb[:, dr_idx][:, :, :, dc_idx]
    tab = jnp.where(col_ok[None, None, None], tab.astype(F32), NEG_BIG)
    tab = tab.transpose(1, 0, 3, 2, 4)
    return tab.reshape(n_pat, rpb.shape[0], GRID_W, kh * GRID_W)


def _nbr_attn_kernel(q_ref, k_ref, v_ref, kc_ref, vc_ref, bias_ref, o_ref, *, rows, kh, scale):
    kc = kc_ref[...].astype(BF16)
    vc = vc_ref[...].astype(BF16)
    nband = kh * GRID_W

    nr = 4 if rows % 4 == 0 else 1
    dn_t = (((1,), (1,)), ((), ()))

    def row_group(g, carry):
        q0 = pl.multiple_of(g * (nr * GRID_W), nr * GRID_W)
        q = q_ref[pl.ds(q0, nr * GRID_W), :].astype(BF16)
        s_ctx = lax.dot_general(q, kc, dn_t, preferred_element_type=F32) * scale
        s_loc, vbs = [], []
        for j in range(nr):
            r = g * nr + j
            rs = jnp.clip(r - WIN_ROWS // 2, 0, rows - kh)
            k0 = pl.multiple_of(rs * GRID_W, GRID_W)
            kb = k_ref[pl.ds(k0, nband), :].astype(BF16)
            vbs.append(v_ref[pl.ds(k0, nband), :].astype(BF16))
            s = lax.dot_general(q[j * GRID_W:(j + 1) * GRID_W], kb, dn_t, preferred_element_type=F32)
            s_loc.append(s * scale + bias_ref[rs - r + WIN_ROWS - 1])
        m_ctx = jnp.max(s_ctx, axis=-1, keepdims=True)
        p_loc, p_ctx = [], []
        for j in range(nr):
            sl = slice(j * GRID_W, (j + 1) * GRID_W)
            m = jnp.maximum(jnp.max(s_loc[j], axis=-1, keepdims=True), m_ctx[sl])
            e_loc = jnp.exp(s_loc[j] - m)
            e_ctx = jnp.exp(s_ctx[sl] - m)
            den = jnp.sum(e_loc, axis=-1, keepdims=True) + jnp.sum(e_ctx, axis=-1, keepdims=True)
            p_loc.append((e_loc / den).astype(BF16))
            p_ctx.append((e_ctx / den).astype(BF16))
        o_ctx = jnp.dot(jnp.concatenate(p_ctx, axis=0), vc, preferred_element_type=F32)
        for j in range(nr):
            o = jnp.dot(p_loc[j], vbs[j], preferred_element_type=F32) + o_ctx[j * GRID_W:(j + 1) * GRID_W]
            o_ref[pl.ds(q0 + j * GRID_W, GRID_W), :] = o.astype(o_ref.dtype)
        return carry

    lax.fori_loop(0, rows // nr, row_group, 0)


def _neighbourhood_attention(pa, k_ctx, v_ctx, bias_tab, B, N, H, dh):
    rows = N // GRID_W
    kh = min(WIN_ROWS, rows)
    Lc = k_ctx.shape[0] // B
    n_pat = bias_tab.shape[0]
    spec = lambda off: pl.BlockSpec((N, dh), lambda b, h: (b, off + h))
    cspec = pl.BlockSpec((Lc, dh), lambda b, h: (b, h))
    return pl.pallas_call(
        functools.partial(_nbr_attn_kernel, rows=rows, kh=kh, scale=dh ** -0.5),
        grid=(B, H),
        in_specs=[spec(0), spec(H), spec(2 * H), cspec, cspec,
                  pl.BlockSpec((n_pat, None, GRID_W, kh * GRID_W), lambda b, h: (0, h, 0, 0))],
        out_specs=pl.BlockSpec((N, dh), lambda b, h: (b, h)),
        out_shape=jax.ShapeDtypeStruct((B * N, H * dh), BF16),
        compiler_params=_cparams(2),
        name="neighbourhood_attention",
    )(pa, pa, pa, k_ctx, v_ctx, bias_tab)


def _rwkv_pre_kernel(p_ref, prev_ref, next_ref, mu_ref, w0_ref, w2_ref, a0_ref, a2_ref, g2_ref,
                     r_ref, k_ref, v_ref, wp_ref, ap_ref, g_ref, *, tiles_per_seq, DR, LW, LA):
    i = pl.program_id(0)
    p = p_ref[...]
    tl = p.shape[0]
    row = lax.broadcasted_iota(jnp.int32, p.shape, 0)
    first = (i % tiles_per_seq) == 0
    last = (i % tiles_per_seq) == tiles_per_seq - 1
    halo_prev = jnp.where(first, 0.0, prev_ref[SUBLANES - 1:SUBLANES, :])
    halo_next = jnp.where(last, 0.0, next_ref[0:1, :])
    prev = jnp.where(row == 0, halo_prev, pltpu.roll(p, 1, 0))
    nxt = jnp.where(row == tl - 1, halo_next, pltpu.roll(p, tl - 1, 0))
    xr = p + mu_ref[...] * (0.5 * (prev + nxt) - p)
    r_ref[...] = xr[:, 0:DR]
    k_ref[...] = xr[:, DR:2 * DR]
    v_ref[...] = xr[:, 2 * DR:3 * DR]
    o = 3 * DR
    tw = jnp.tanh(xr[:, o:o + LW]).astype(BF16)
    xa = xr[:, o + LW:o + LW + LA].astype(BF16)
    sg = _sigmoid(xr[:, o + LW + LA:]).astype(BF16)
    for d in range(2):
        wp_ref[d] = w0_ref[d] + jnp.dot(tw, w2_ref[d], preferred_element_type=F32)
        ap_ref[d] = a0_ref[d] + jnp.dot(xa, a2_ref[d], preferred_element_type=F32)
    g_ref[...] = jnp.dot(sg, g2_ref[...], preferred_element_type=F32)


def _rwkv_pre(pr, L, mu, w0, w2, a0, a2, g2, DR, LW, LA):
    T, W = pr.shape
    tl = _tile(L, 128, SUBLANES)
    tiles_per_seq = L // tl
    hb = tl // SUBLANES
    nhb = T // SUBLANES
    LG = g2.shape[0]
    full = lambda shape: pl.BlockSpec(shape, lambda i: (0,) * len(shape))
    out_main = pl.BlockSpec((tl, DR), lambda i: (i, 0))
    out_dir = pl.BlockSpec((2, tl, DR), lambda i: (0, i, 0))
    return pl.pallas_call(
        functools.partial(_rwkv_pre_kernel, tiles_per_seq=tiles_per_seq, DR=DR, LW=LW, LA=LA),
        grid=(T // tl,),
        in_specs=[pl.BlockSpec((tl, W), lambda i: (i, 0)),
                  pl.BlockSpec((SUBLANES, W), lambda i: (jnp.maximum(i * hb - 1, 0), 0)),
                  pl.BlockSpec((SUBLANES, W), lambda i: (jnp.minimum((i + 1) * hb, nhb - 1), 0)),
                  full((1, W)), full((2, 1, DR)), full((2, LW, DR)), full((2, 1, DR)),
                  full((2, LA, DR)), full((LG, DR))],
        out_specs=[out_main, out_main, out_main, out_dir, out_dir, out_main],
        out_shape=[jax.ShapeDtypeStruct((T, DR), F32)] * 3
                  + [jax.ShapeDtypeStruct((2, T, DR), F32)] * 2
                  + [jax.ShapeDtypeStruct((T, DR), F32)],
        compiler_params=_cparams(1),
        name="rwkv_pre",
    )(pr, pr, pr, mu, w0, w2, a0, a2, g2)


def _scan_kernel(r_ref, k_ref, v_ref, wp_ref, ap_ref, kk_w_ref, ka_w_ref, rk_w_ref, lw_ref, lb_ref,
                 s0_ref, o_ref, sT_ref, S_scr, y_scr, *, tb, nt, n):
    d = pl.program_id(0)
    tblk = pl.program_id(2)

    @pl.when(tblk == 0)
    def _():
        S_scr[...] = s0_ref[...]

    k_k = kk_w_ref[...]
    k_a = ka_w_ref[...]
    r_k = rk_w_ref[...]
    lnw = lw_ref[...]
    lnb = lb_ref[...]

    def step(i, carry):
        t = i + d * (tb - 1 - 2 * i)
        r_t = r_ref[t]
        k_t = k_ref[t]
        v_t = v_ref[t]
        wl = -_softplus(-wp_ref[t]) - 0.5
        decay = jnp.exp(-jnp.exp(wl))
        a = _sigmoid(ap_ref[t])
        kk = k_t * k_k
        kk = kk / jnp.maximum(jnp.sqrt(jnp.sum(kk * kk, axis=0, keepdims=True)), 1e-12)
        kd = k_t * (1.0 + (a - 1.0) * k_a)
        b = kk * a
        for v in range(n):
            Sv = S_scr[v]
            sa = -jnp.sum(Sv * kk, axis=0, keepdims=True)
            Sv = Sv * decay + sa * b + v_t[v:v + 1, :] * kd
            S_scr[v] = Sv
            y_scr[v:v + 1, :] = jnp.sum(Sv * r_t, axis=0, keepdims=True)
        y = y_scr[...]
        mean = jnp.mean(y, axis=0, keepdims=True)
        yc = y - mean
        var = jnp.mean(yc * yc, axis=0, keepdims=True)
        bonus = jnp.sum(r_t * kd * r_k, axis=0, keepdims=True) * v_t
        o_ref[t] = yc * lax.rsqrt(var + LNX_EPS) * lnw + lnb + bonus
        return carry

    lax.fori_loop(0, tb, step, 0)

    @pl.when(tblk == nt - 1)
    def _():
        sT_ref[...] = S_scr[...]


def _rwkv_scan(r, k, v, wp, ap, k_k, k_a, r_k, lnw, lnb, s0):
    L, n, NC = r.shape
    tb = _tile(L, 32, 1)
    nt = L // tb
    tmap = lambda d, t: t + d * (nt - 1 - 2 * t)
    seq = pl.BlockSpec((tb, n, LANES), lambda d, c, t: (tmap(d, t), 0, c))
    seq_d = pl.BlockSpec((None, tb, n, LANES), lambda d, c, t: (d, tmap(d, t), 0, c))
    wgt = pl.BlockSpec((n, LANES), lambda d, c, t: (0, c))
    st = pl.BlockSpec((None, n, n, LANES), lambda d, c, t: (d, 0, 0, c))
    return pl.pallas_call(
        functools.partial(_scan_kernel, tb=tb, nt=nt, n=n),
        grid=(2, NC // LANES, nt),
        in_specs=[seq, seq, seq, seq_d, seq_d, wgt, wgt, wgt, wgt, wgt, st],
        out_specs=[seq_d, st],
        out_shape=[jax.ShapeDtypeStruct((2, L, n, NC), F32),
                   jax.ShapeDtypeStruct((2, n, n, NC), F32)],
        scratch_shapes=[pltpu.VMEM((n, n, LANES), F32), pltpu.VMEM((n, LANES), F32)],
        compiler_params=_cparams(3),
        name="rwkv_scan",
    )(r, k, v, wp, ap, k_k, k_a, r_k, lnw, lnb, s0)


def _gate_kernel(of_ref, ob_ref, g_ref, o_ref):
    o_ref[...] = ((of_ref[...] + ob_ref[...]) * g_ref[...]).astype(o_ref.dtype)


def _rwkv_gate(o_dirs, g):
    _, T, DR = o_dirs.shape
    tm = _tile(T, 512, SUBLANES)
    dspec = lambda d: pl.BlockSpec((None, tm, DR), lambda i: (d, i, 0))
    return pl.pallas_call(
        _gate_kernel,
        grid=(T // tm,),
        in_specs=[dspec(0), dspec(1), pl.BlockSpec((tm, DR), lambda i: (i, 0))],
        out_specs=pl.BlockSpec((tm, DR), lambda i: (i, 0)),
        out_shape=jax.ShapeDtypeStruct((T, DR), BF16),
        compiler_params=_cparams(1),
        name="rwkv_gate",
    )(o_dirs, o_dirs, g)


def _to_chain_layout(x, B, L, H, n, NCp):
    lead = x.shape[:-2]
    nl = len(lead)
    x = x.reshape(lead + (B, L, H, n))
    x = jnp.transpose(x, tuple(range(nl)) + (nl + 1, nl + 3, nl, nl + 2))
    x = x.reshape(lead + (L, n, B * H))
    if NCp != B * H:
        x = jnp.pad(x, [(0, 0)] * (nl + 2) + [(0, NCp - B * H)])
    return x


def _from_chain_layout(x, B, L, H, n):
    x = x[..., :B * H].reshape(2, L, n, B, H)
    return jnp.transpose(x, (0, 3, 1, 4, 2)).reshape(2, B * L, H * n)


def _chain_weight(w, B, H, n, NCp):
    t = jnp.tile(w.reshape(H, n).T, (1, B))
    if NCp != B * H:
        t = jnp.pad(t, [(0, 0), (0, NCp - B * H)])
    return t


def _rwkv_branch(pr, B, L, H, n, s_f0, s_b0, rw):
    DR = H * n
    NC = B * H
    NCp = _round_up(NC, LANES)
    r, k, v, wp, ap, g = _rwkv_pre(pr, L, rw["mu"], rw["w0"], rw["w2"], rw["a0"], rw["a2"], rw["g2"],
                                   DR, rw["LW"], rw["LA"])
    tc = functools.partial(_to_chain_layout, B=B, L=L, H=H, n=n, NCp=NCp)
    cw = functools.partial(_chain_weight, B=B, H=H, n=n, NCp=NCp)
    s0 = jnp.stack([s_f0, s_b0]).astype(F32)
    s0 = jnp.transpose(s0, (0, 3, 4, 1, 2)).reshape(2, n, n, NC)
    if NCp != NC:
        s0 = jnp.pad(s0, [(0, 0)] * 3 + [(0, NCp - NC)])
    o_dirs, sT = _rwkv_scan(tc(r), tc(k), tc(v), tc(wp), tc(ap),
                            cw(rw["k_k"]), cw(rw["k_a"]), cw(rw["r_k"]), cw(rw["lnx_w"]), cw(rw["lnx_b"]), s0)
    o_rwkv = _rwkv_gate(_from_chain_layout(o_dirs, B, L, H, n), g)
    sT = jnp.transpose(sT[..., :NC].reshape(2, n, n, B, H), (0, 3, 4, 1, 2))
    return o_rwkv, sT[0], sT[1]


def _pad_rows(w, rows):
    return jnp.pad(w, [(0, 0)] * (w.ndim - 2) + [(0, rows - w.shape[-2]), (0, 0)])


def kernel(x_prompt, x_sample, cache_attn_k, cache_attn_v, state_rwkv_fwd, state_rwkv_bwd, c, c_ctx,
           w_mod, b_mod, norm1_w, norm2_w, norm_f_w, w_in, tshift_mu, attn_rpb,
           rwkv_w0, rwkv_w2, rwkv_a0, rwkv_a2, rwkv_g2, rwkv_k_k, rwkv_k_a, rwkv_r_k, lnx_w, lnx_b,
           w_o_attn, w_o_rwkv, w_out, w_mlp1, w_mlp2):
    Bc, Lc, D = x_prompt.shape
    Bs, Ls, _ = x_sample.shape
    depth = w_mod.shape[0]
    HA, dh = cache_attn_k.shape[3], cache_attn_k.shape[4]
    HR, n = state_rwkv_fwd.shape[2], state_rwkv_fwd.shape[3]
    DA, DR = HA * dh, HR * n
    lw, la, lg = rwkv_w2.shape[2], rwkv_a2.shape[2], rwkv_g2.shape[1]
    LW, LA, LG = _round_up(lw, LANES), _round_up(la, LANES), _round_up(lg, LANES)
    past = cache_attn_k.shape[2]
    rows = Ls // GRID_W

    xc = x_prompt.reshape(Bc * Lc, D)
    xs = x_sample.reshape(Bs * Ls, D)
    cond_rows = _round_up(Bs + 1, SUBLANES)
    cond = jnp.zeros((cond_rows, D), F32).at[:Bs].set(c).at[Bs].set(c_ctx)
    pad_c = lambda a, wdt: jnp.pad(a, [(0, 0)] * (a.ndim - 1) + [(0, wdt - a.shape[-1])])

    new_k, new_v, new_sf, new_sb = [], [], [], []
    for l in range(depth):
        mod3 = _adaln(cond, w_mod[l], b_mod[l]).reshape(cond_rows * 6, 1, D)

        wi = w_in[l]
        o1, o2 = 3 * DA, 3 * DA + 3 * DR
        w_a = wi[:, :o1].astype(BF16)
        w_r = jnp.concatenate([wi[:, o1:o2], pad_c(wi[:, o2:o2 + lw], LW),
                               pad_c(wi[:, o2 + lw:o2 + lw + la], LA),
                               pad_c(wi[:, o2 + lw + la:o2 + lw + la + lg], LG)], axis=1).astype(BF16)
        w_g = wi[:, o2 + lw + la + lg:].astype(BF16)
        mu = tshift_mu[l]
        mu_r = jnp.concatenate([mu[:3 * DR], pad_c(mu[3 * DR:3 * DR + lw], LW),
                                pad_c(mu[3 * DR + lw:3 * DR + lw + la], LA),
                                pad_c(mu[3 * DR + lw + la:], LG)]).reshape(1, -1)
        rw = dict(mu=mu_r, LW=LW, LA=LA,
                  w0=rwkv_w0[l].reshape(2, 1, DR), w2=_pad_rows(rwkv_w2[l], LW).astype(BF16),
                  a0=rwkv_a0[l].reshape(2, 1, DR), a2=_pad_rows(rwkv_a2[l], LA).astype(BF16),
                  g2=_pad_rows(rwkv_g2[l], LG).astype(BF16),
                  k_k=rwkv_k_k[l], k_a=rwkv_k_a[l], r_k=rwkv_r_k[l].reshape(-1),
                  lnx_w=lnx_w[l], lnx_b=lnx_b[l])
        w_oa, w_or = w_o_attn[l].astype(BF16), w_o_rwkv[l].astype(BF16)
        w_o, w_1, w_2 = w_out[l].astype(BF16), w_mlp1[l].astype(BF16), w_mlp2[l].astype(BF16)
        bias_tab = _nbr_bias_table(attn_rpb[l], rows)

        def block(x, B, L, rows_per_cond, cond_base, attention, s_f0, s_b0, pa_dtype):
            h = _norm_modulate(x, norm1_w[l], mod3, (0, 1), rows_per_cond, cond_base)
            pa = _matmul(h, w_a, pa_dtype, name="proj_attn")
            pr = _matmul(h, w_r, F32, tn_pref=512, name="proj_rwkv")
            pg = _matmul(h, w_g, F32, name="proj_gate")
            o_attn = attention(pa)
            o_rwkv, s_f, s_b = _rwkv_branch(pr, B, L, HR, n, s_f0, s_b0, rw)
            merged = _merge(o_attn, w_oa, o_rwkv, w_or, pg)
            x = _matmul_gated_residual(merged, w_o, x, mod3, 2, rows_per_cond, cond_base, "out_proj")
            h = _norm_modulate(x, norm2_w[l], mod3, (3, 4), rows_per_cond, cond_base)
            a = _matmul(h, w_1, BF16, relu2=True, name="mlp_up")
            x = _matmul_gated_residual(a, w_2, x, mod3, 5, rows_per_cond, cond_base, "mlp_down")
            return x, pa, s_f, s_b

        z = jnp.zeros((Bc, HR, n, n), F32)
        xc, pa_c, s_f, s_b = block(
            xc, Bc, Lc, Bc * Lc, Bs,
            lambda pa: _context_attention(pa, Bc, Lc, HA, dh), z, z, F32)
        new_k.append(pa_c[:, DA:2 * DA].reshape(Bc, Lc, HA, dh))
        new_v.append(pa_c[:, 2 * DA:].reshape(Bc, Lc, HA, dh))
        new_sf.append(s_f)
        new_sb.append(s_b)

        kc = cache_attn_k[:, l].reshape(Bs * past, DA)
        vc = cache_attn_v[:, l].reshape(Bs * past, DA)
        xs, _, _, _ = block(
            xs, Bs, Ls, Ls, 0,
            lambda pa: _neighbourhood_attention(pa, kc, vc, bias_tab, Bs, Ls, HA, dh),
            state_rwkv_fwd[:, l], state_rwkv_bwd[:, l], BF16)

    y_prompt = _final_norm(xc, norm_f_w).reshape(Bc, Lc, D)
    y_sample = _final_norm(xs, norm_f_w).reshape(Bs, Ls, D)
    return (y_prompt, y_sample, jnp.stack(new_k, axis=1), jnp.stack(new_v, axis=1),
            jnp.stack(new_sf, axis=1), jnp.stack(new_sb, axis=1))
```

```python
import functools

import numpy as np
import jax
import jax.numpy as jnp
from jax import lax
from jax.experimental import pallas as pl
from jax.experimental.pallas import tpu as pltpu

GRID_W = 64
WIN_ROWS = 8
WIN_COLS = 16
RMS_EPS = 1e-6
LNX_EPS = 64e-5

LANES = 128
SUBLANES = 8
VMEM_LIMIT_BYTES = 56 * 1024 * 1024

F32 = jnp.float32
BF16 = jnp.bfloat16
NEG_BIG = -1e30


def _cparams(n_axes):
    return pltpu.CompilerParams(dimension_semantics=("arbitrary",) * n_axes,
                                vmem_limit_bytes=VMEM_LIMIT_BYTES)


def _tile(n, pref, unit=LANES):
    if n <= pref:
        return n
    t = (pref // unit) * unit
    while t > unit and n % t:
        t -= unit
    assert n % t == 0, (n, pref, unit)
    return t


def _round_up(n, m):
    return (n + m - 1) // m * m


def _sigmoid(x):
    return 1.0 / (1.0 + jnp.exp(-x))


def _softplus(x):
    return jnp.maximum(x, 0.0) + jnp.log(1.0 + jnp.exp(-jnp.abs(x)))


def _mod_kernel(c_ref, w_ref, b_ref, o_ref):
    c = c_ref[...]
    s = (c * _sigmoid(c)).astype(BF16)
    o_ref[...] = jnp.dot(s, w_ref[...].astype(BF16), preferred_element_type=F32) + b_ref[...]


def _adaln(cond, w_mod, b_mod):
    R, D = cond.shape
    N = w_mod.shape[1]
    tn = _tile(N, 512)
    return pl.pallas_call(
        _mod_kernel,
        grid=(N // tn,),
        in_specs=[pl.BlockSpec((R, D), lambda j: (0, 0)),
                  pl.BlockSpec((D, tn), lambda j: (0, j)),
                  pl.BlockSpec((1, tn), lambda j: (0, j))],
        out_specs=pl.BlockSpec((R, tn), lambda j: (0, j)),
        out_shape=jax.ShapeDtypeStruct((R, N), F32),
        compiler_params=_cparams(1),
        name="adaln",
    )(cond, w_mod, b_mod.reshape(1, N))


def _norm_mod_kernel(x_ref, w_ref, sh_ref, sc_ref, o_ref):
    x = x_ref[...]
    y = x * lax.rsqrt(jnp.mean(x * x, axis=-1, keepdims=True) + RMS_EPS) * w_ref[...]
    o_ref[...] = (y * (1.0 + sc_ref[...]) + sh_ref[...]).astype(o_ref.dtype)


def _norm_kernel(x_ref, w_ref, o_ref):
    x = x_ref[...]
    y = x * lax.rsqrt(jnp.mean(x * x, axis=-1, keepdims=True) + RMS_EPS) * w_ref[...]
    o_ref[...] = y.astype(o_ref.dtype)


def _mod_spec(part, tm, rows_per_cond, cond_base, D):
    return pl.BlockSpec((None, 1, D),
                        lambda i, *_: ((cond_base + (i * tm) // rows_per_cond) * 6 + part, 0, 0))


def _norm_modulate(x, w, mod3, parts, rows_per_cond, cond_base):
    T, D = x.shape
    tm = _tile(min(T, rows_per_cond), 256, SUBLANES)
    return pl.pallas_call(
        _norm_mod_kernel,
        grid=(T // tm,),
        in_specs=[pl.BlockSpec((tm, D), lambda i: (i, 0)),
                  pl.BlockSpec((1, D), lambda i: (0, 0)),
                  _mod_spec(parts[0], tm, rows_per_cond, cond_base, D),
                  _mod_spec(parts[1], tm, rows_per_cond, cond_base, D)],
        out_specs=pl.BlockSpec((tm, D), lambda i: (i, 0)),
        out_shape=jax.ShapeDtypeStruct((T, D), BF16),
        compiler_params=_cparams(1),
        name="norm_modulate",
    )(x, w.reshape(1, D), mod3, mod3)


def _final_norm(x, w):
    T, D = x.shape
    tm = _tile(T, 256, SUBLANES)
    return pl.pallas_call(
        _norm_kernel,
        grid=(T // tm,),
        in_specs=[pl.BlockSpec((tm, D), lambda i: (i, 0)),
                  pl.BlockSpec((1, D), lambda i: (0, 0))],
        out_specs=pl.BlockSpec((tm, D), lambda i: (i, 0)),
        out_shape=jax.ShapeDtypeStruct((T, D), F32),
        compiler_params=_cparams(1),
        name="final_norm",
    )(x, w.reshape(1, D))


def _mm_kernel(x_ref, w_ref, o_ref):
    o_ref[...] = jnp.dot(x_ref[...], w_ref[...], preferred_element_type=F32).astype(o_ref.dtype)


def _mm_relu2_kernel(x_ref, w_ref, o_ref):
    a = jnp.maximum(jnp.dot(x_ref[...], w_ref[...], preferred_element_type=F32), 0.0)
    o_ref[...] = (a * a).astype(o_ref.dtype)


def _matmul(x, w, out_dtype, *, relu2=False, tm_pref=1024, tn_pref=1024, name="matmul"):
    M, K = x.shape
    N = w.shape[1]
    tm, tn = _tile(M, tm_pref, SUBLANES), _tile(N, tn_pref)
    return pl.pallas_call(
        _mm_relu2_kernel if relu2 else _mm_kernel,
        grid=(M // tm, N // tn),
        in_specs=[pl.BlockSpec((tm, K), lambda i, j: (i, 0)),
                  pl.BlockSpec((K, tn), lambda i, j: (0, j))],
        out_specs=pl.BlockSpec((tm, tn), lambda i, j: (i, j)),
        out_shape=jax.ShapeDtypeStruct((M, N), out_dtype),
        compiler_params=_cparams(2),
        name=name,
    )(x, w)


def _merge_kernel(oa_ref, wa_ref, or_ref, wr_ref, ga_ref, gr_ref, o_ref):
    ya = jnp.dot(oa_ref[...], wa_ref[...], preferred_element_type=F32)
    yr = jnp.dot(or_ref[...], wr_ref[...], preferred_element_type=F32)
    o_ref[...] = (_sigmoid(ga_ref[...]) * ya + _sigmoid(gr_ref[...]) * yr).astype(o_ref.dtype)


def _merge(o_attn, w_oa, o_rwkv, w_or, pg):
    T, DA = o_attn.shape
    DR = o_rwkv.shape[1]
    D = w_oa.shape[1]
    tm, tn = _tile(T, 512, SUBLANES), _tile(D, 1024)
    nj = D // tn
    return pl.pallas_call(
        _merge_kernel,
        grid=(T // tm, nj),
        in_specs=[pl.BlockSpec((tm, DA), lambda i, j: (i, 0)),
                  pl.BlockSpec((DA, tn), lambda i, j: (0, j)),
                  pl.BlockSpec((tm, DR), lambda i, j: (i, 0)),
                  pl.BlockSpec((DR, tn), lambda i, j: (0, j)),
                  pl.BlockSpec((tm, tn), lambda i, j: (i, j)),
                  pl.BlockSpec((tm, tn), lambda i, j: (i, j + nj))],
        out_specs=pl.BlockSpec((tm, tn), lambda i, j: (i, j)),
        out_shape=jax.ShapeDtypeStruct((T, D), BF16),
        compiler_params=_cparams(2),
        name="merge",
    )(o_attn, w_oa, o_rwkv, w_or, pg, pg)


def _mm_resid_kernel(x_ref, w_ref, res_ref, gate_ref, o_ref, acc_ref, *, nk):
    k = pl.program_id(2)
    part = jnp.dot(x_ref[...], w_ref[...], preferred_element_type=F32)

    @pl.when(k == 0)
    def _():
        acc_ref[...] = part

    @pl.when(k > 0)
    def _():
        acc_ref[...] += part

    @pl.when(k == nk - 1)
    def _():
        o_ref[...] = res_ref[...] + gate_ref[...] * acc_ref[...]


def _matmul_gated_residual(x, w, res, mod3, part, rows_per_cond, cond_base, name):
    M, K = x.shape
    N = w.shape[1]
    tm = _tile(min(M, rows_per_cond), 1024, SUBLANES)
    tn, tk = _tile(N, 1024), _tile(K, 2048)
    nk = K // tk
    gate_spec = pl.BlockSpec(
        (None, 1, tn), lambda i, j, k: ((cond_base + (i * tm) // rows_per_cond) * 6 + part, 0, j))
    return pl.pallas_call(
        functools.partial(_mm_resid_kernel, nk=nk),
        grid=(M // tm, N // tn, nk),
        in_specs=[pl.BlockSpec((tm, tk), lambda i, j, k: (i, k)),
                  pl.BlockSpec((tk, tn), lambda i, j, k: (k, j)),
                  pl.BlockSpec((tm, tn), lambda i, j, k: (i, j)),
                  gate_spec],
        out_specs=pl.BlockSpec((tm, tn), lambda i, j, k: (i, j)),
        out_shape=jax.ShapeDtypeStruct((M, N), F32),
        scratch_shapes=[pltpu.VMEM((tm, tn), F32)],
        compiler_params=_cparams(3),
        name=name,
    )(x, w, res, mod3)


def _ctx_attn_kernel(q_ref, k_ref, v_ref, o_ref, *, scale):
    q = q_ref[...].astype(BF16)
    k = k_ref[...].astype(BF16)
    s = lax.dot_general(q, k, (((1,), (1,)), ((), ())), preferred_element_type=F32) * scale
    m = jnp.max(s, axis=-1, keepdims=True)
    e = jnp.exp(s - m)
    p = (e / jnp.sum(e, axis=-1, keepdims=True)).astype(BF16)
    o_ref[...] = jnp.dot(p, v_ref[...].astype(BF16), preferred_element_type=F32).astype(o_ref.dtype)


def _context_attention(pa, B, L, H, dh):
    spec = lambda off: pl.BlockSpec((L, dh), lambda b, h: (b, off + h))
    return pl.pallas_call(
        functools.partial(_ctx_attn_kernel, scale=dh ** -0.5),
        grid=(B, H),
        in_specs=[spec(0), spec(H), spec(2 * H)],
        out_specs=pl.BlockSpec((L, dh), lambda b, h: (b, h)),
        out_shape=jax.ShapeDtypeStruct((B * L, H * dh), BF16),
        compiler_params=_cparams(2),
        name="context_attention",
    )(pa, pa, pa)


def _nbr_bias_table(rpb, rows):
    kh = min(WIN_ROWS, rows)
    qc = np.arange(GRID_W)[:, None]
    kc = np.arange(GRID_W)[None, :]
    cs = np.clip(qc - WIN_COLS // 2, 0, GRID_W - WIN_COLS)
    col_ok = (kc >= cs) & (kc < cs + WIN_COLS)
    dc_idx = np.clip(kc - qc + WIN_COLS - 1, 0, 2 * WIN_COLS - 2)
    n_pat = 2 * WIN_ROWS - kh
    dr_idx = np.arange(n_pat)[:, None] + np.arange(kh)[None, :]
    tab = rpb[:, dr_idx][:, :, :, dc_idx]
    tab = jnp.where(col_ok[None, None, None], tab.astype(F32), NEG_BIG)
    tab = tab.transpose(1, 0, 3, 2, 4)
    return tab.reshape(n_pat, rpb.shape[0], GRID_W, kh * GRID_W)


def _nbr_attn_kernel(q_ref, k_ref, v_ref, kc_ref, vc_ref, bias_ref, o_ref, *, rows, kh, scale):
    kc = kc_ref[...].astype(BF16)
    vc = vc_ref[...].astype(BF16)
    nband = kh * GRID_W

    nr = 4 if rows % 4 == 0 else 1
    dn_t = (((1,), (1,)), ((), ()))

    def row_group(g, carry):
        q0 = pl.multiple_of(g * (nr * GRID_W), nr * GRID_W)
        q = q_ref[pl.ds(q0, nr * GRID_W), :].astype(BF16)
        s_ctx = lax.dot_general(q, kc, dn_t, preferred_element_type=F32) * scale
        s_loc, vbs = [], []
        for j in range(nr):
            r = g * nr + j
            rs = jnp.clip(r - WIN_ROWS // 2, 0, rows - kh)
            k0 = pl.multiple_of(rs * GRID_W, GRID_W)
            kb = k_ref[pl.ds(k0, nband), :].astype(BF16)
            vbs.append(v_ref[pl.ds(k0, nband), :].astype(BF16))
            s = lax.dot_general(q[j * GRID_W:(j + 1) * GRID_W], kb, dn_t, preferred_element_type=F32)
            s_loc.append(s * scale + bias_ref[rs - r + WIN_ROWS - 1])
        m_ctx = jnp.max(s_ctx, axis=-1, keepdims=True)
        p_loc, p_ctx = [], []
        for j in range(nr):
            sl = slice(j * GRID_W, (j + 1) * GRID_W)
            m = jnp.maximum(jnp.max(s_loc[j], axis=-1, keepdims=True), m_ctx[sl])
            e_loc = jnp.exp(s_loc[j] - m)
            e_ctx = jnp.exp(s_ctx[sl] - m)
            den = jnp.sum(e_loc, axis=-1, keepdims=True) + jnp.sum(e_ctx, axis=-1, keepdims=True)
            p_loc.append((e_loc / den).astype(BF16))
            p_ctx.append((e_ctx / den).astype(BF16))
        o_ctx = jnp.dot(jnp.concatenate(p_ctx, axis=0), vc, preferred_element_type=F32)
        for j in range(nr):
            o = jnp.dot(p_loc[j], vbs[j], preferred_element_type=F32) + o_ctx[j * GRID_W:(j + 1) * GRID_W]
            o_ref[pl.ds(q0 + j * GRID_W, GRID_W), :] = o.astype(o_ref.dtype)
        return carry

    lax.fori_loop(0, rows // nr, row_group, 0)


def _neighbourhood_attention(pa, k_ctx, v_ctx, bias_tab, B, N, H, dh):
    rows = N // GRID_W
    kh = min(WIN_ROWS, rows)
    Lc = k_ctx.shape[0] // B
    n_pat = bias_tab.shape[0]
    spec = lambda off: pl.BlockSpec((N, dh), lambda b, h: (b, off + h))
    cspec = pl.BlockSpec((Lc, dh), lambda b, h: (b, h))
    return pl.pallas_call(
        functools.partial(_nbr_attn_kernel, rows=rows, kh=kh, scale=dh ** -0.5),
        grid=(B, H),
        in_specs=[spec(0), spec(H), spec(2 * H), cspec, cspec,
                  pl.BlockSpec((n_pat, None, GRID_W, kh * GRID_W), lambda b, h: (0, h, 0, 0))],
        out_specs=pl.BlockSpec((N, dh), lambda b, h: (b, h)),
        out_shape=jax.ShapeDtypeStruct((B * N, H * dh), BF16),
        compiler_params=_cparams(2),
        name="neighbourhood_attention",
    )(pa, pa, pa, k_ctx, v_ctx, bias_tab)


def _head_sum(x, H):
    nt = x.shape[1] // LANES
    s = x[:, 0:LANES]
    for j in range(1, nt):
        s = s + x[:, j * LANES:(j + 1) * LANES]
    span = LANES // 2
    while span >= H:
        s = s + pltpu.roll(s, span, 1)
        span //= 2
    return jnp.concatenate([s] * nt, axis=1)


def _rwkv_pre_kernel(p_ref, prev_ref, next_ref, mu_ref, w0_ref, w2_ref, a0_ref, a2_ref, g2_ref,
                     kkw_ref, kaw_ref, rkw_ref,
                     r_ref, v_ref, kk_ref, dec_ref, b_ref, kd_ref, g_ref, bonus_ref,
                     *, tiles_per_seq, DR, LW, LA, H):
    i = pl.program_id(0)
    p = p_ref[...]
    first = (i % tiles_per_seq) == 0
    last = (i % tiles_per_seq) == tiles_per_seq - 1
    halo_prev = jnp.where(first, 0.0, prev_ref[...])
    halo_next = jnp.where(last, 0.0, next_ref[...])
    prev = jnp.concatenate([halo_prev, p[:-SUBLANES]], axis=0)
    nxt = jnp.concatenate([p[SUBLANES:], halo_next], axis=0)
    xr = p + mu_ref[...] * (0.5 * (prev + nxt) - p)
    r = xr[:, 0:DR]
    k = xr[:, DR:2 * DR]
    v = xr[:, 2 * DR:3 * DR]
    o = 3 * DR
    tw = jnp.tanh(xr[:, o:o + LW]).astype(BF16)
    xa = xr[:, o + LW:o + LW + LA].astype(BF16)
    sg = _sigmoid(xr[:, o + LW + LA:]).astype(BF16)
    kk = k * kkw_ref[...]
    kk = kk / jnp.maximum(jnp.sqrt(_head_sum(kk * kk, H)), 1e-12)
    r_ref[...] = r
    v_ref[...] = v
    kk_ref[...] = kk
    g_ref[...] = jnp.dot(sg, g2_ref[...], preferred_element_type=F32)
    rk = r * rkw_ref[...]
    bonus = None
    for d in range(2):
        wl = w0_ref[d] + jnp.dot(tw, w2_ref[d], preferred_element_type=F32)
        wl = -_softplus(-wl) - 0.5
        dec_ref[d] = jnp.exp(-jnp.exp(wl))
        a = _sigmoid(a0_ref[d] + jnp.dot(xa, a2_ref[d], preferred_element_type=F32))
        kd = k * (1.0 + (a - 1.0) * kaw_ref[...])
        kd_ref[d] = kd
        b_ref[d] = kk * a
        s = _head_sum(rk * kd, H)
        bonus = s if bonus is None else bonus + s
    bonus_ref[...] = bonus * v


def _rwkv_pre(pr, L, rw, DR, H):
    T, W = pr.shape
    hb = _tile(L, 16, 1)
    tl = hb * SUBLANES
    tiles_per_seq = L // hb
    nhb = T // SUBLANES
    LW, LA = rw["LW"], rw["LA"]
    LG = rw["g2"].shape[0]
    full = lambda shape: pl.BlockSpec(shape, lambda i: (0,) * len(shape))
    out_main = pl.BlockSpec((tl, DR), lambda i: (i, 0))
    out_dir = pl.BlockSpec((2, tl, DR), lambda i: (0, i, 0))
    sd_main = jax.ShapeDtypeStruct((T, DR), F32)
    sd_dir = jax.ShapeDtypeStruct((2, T, DR), F32)
    row = lambda w: w.reshape(1, DR)
    return pl.pallas_call(
        functools.partial(_rwkv_pre_kernel, tiles_per_seq=tiles_per_seq, DR=DR, LW=LW, LA=LA, H=H),
        grid=(T // tl,),
        in_specs=[pl.BlockSpec((tl, W), lambda i: (i, 0)),
                  pl.BlockSpec((SUBLANES, W), lambda i: (jnp.maximum(i * hb - 1, 0), 0)),
                  pl.BlockSpec((SUBLANES, W), lambda i: (jnp.minimum((i + 1) * hb, nhb - 1), 0)),
                  full((1, W)), full((2, 1, DR)), full((2, LW, DR)), full((2, 1, DR)),
                  full((2, LA, DR)), full((LG, DR)), full((1, DR)), full((1, DR)), full((1, DR))],
        out_specs=[out_main, out_main, out_main, out_dir, out_dir, out_dir, out_main, out_main],
        out_shape=[sd_main, sd_main, sd_main, sd_dir, sd_dir, sd_dir, sd_main, sd_main],
        compiler_params=_cparams(1),
        name="rwkv_pre",
    )(pr, pr, pr, rw["mu"], rw["w0"], rw["w2"], rw["a0"], rw["a2"], rw["g2"],
      row(rw["k_k"]), row(rw["k_a"]), row(rw["r_k"]))


N_KEY_OPS = 5


def _scan_kernel(kk_ref, dec_ref, b_ref, kd_ref, r_ref, v_ref, e_ref, s0_ref, y_ref, sT_ref,
                 S_scr, opa_scr, opb_scr, nat_scr, lhs_scr, vy_scr, *, tb, nt, n, H):
    VL = LANES // H
    NT = n // VL
    VB = min(4, NT)
    NVG = NT // VB
    QPI = VL // NVG
    d = pl.program_id(0)
    tblk = pl.program_id(2)
    key_refs = (kk_ref, dec_ref, b_ref, kd_ref, r_ref)

    @pl.when(tblk == 0)
    def _():
        S_scr[...] = s0_ref[...]

    def block_rows(i):
        t = i + d * (tb - 1 - 2 * i)
        return pl.ds(pl.multiple_of(t * SUBLANES, SUBLANES), SUBLANES)

    def stage(i):
        rows = block_rows(i)
        for oi, ref in enumerate(key_refs):
            x = ref[rows, :]
            for j in range(NT):
                nat_scr[pl.ds((oi * NT + j) * SUBLANES, SUBLANES), :] = x[:, j * LANES:(j + 1) * LANES]
        x = nat_scr[...]
        x1 = x.astype(BF16)
        r1 = x - x1.astype(F32)
        x2 = r1.astype(BF16)
        x3 = (r1 - x2.astype(F32)).astype(BF16)
        lhs_scr[...] = jnp.concatenate([x1, x2, x3], axis=1)

    def replicate_group(op_dst, q):
        rep = jnp.dot(lhs_scr[...], e_ref[q], preferred_element_type=F32)
        for oi in range(N_KEY_OPS):
            for j in range(NT):
                r0 = (oi * NT + j) * SUBLANES
                op_dst[oi, j * VL + q] = rep[r0:r0 + SUBLANES]

    def one_step(i, op_cur, op_nxt):
        rows = block_rows(i)
        v_t = v_ref[rows, :]
        for j in range(NT):
            vy_scr[0, j] = v_t[:, j * LANES:(j + 1) * LANES]
        stage(jnp.minimum(i + 1, tb - 1))

        def value_group(vg, c):
            for qq in range(QPI):
                replicate_group(op_nxt, vg * QPI + qq)
            vts = [vg * VB + u for u in range(VB)]
            vals = [vy_scr[0, vt] for vt in vts]
            acc = [[None, None] for _ in vts]
            for k in range(n):
                kk_k = op_cur[0, k]
                for u, vt in enumerate(vts):
                    p = S_scr[vt, k] * kk_k
                    acc[u][k % 2] = p if acc[u][k % 2] is None else acc[u][k % 2] + p
            sa = [-(a0 + a1) for a0, a1 in acc]
            acc = [[None, None] for _ in vts]
            for k in range(n):
                w_k, b_k, kd_k, r_kk = op_cur[1, k], op_cur[2, k], op_cur[3, k], op_cur[4, k]
                for u, vt in enumerate(vts):
                    s_new = S_scr[vt, k] * w_k + sa[u] * b_k + vals[u] * kd_k
                    S_scr[vt, k] = s_new
                    p = s_new * r_kk
                    acc[u][k % 2] = p if acc[u][k % 2] is None else acc[u][k % 2] + p
            for u, vt in enumerate(vts):
                vy_scr[1, vt] = acc[u][0] + acc[u][1]
            return c

        lax.fori_loop(0, NVG, value_group, 0)
        y_ref[rows, :] = jnp.concatenate([vy_scr[1, j] for j in range(NT)], axis=1)

    stage(0)
    for q in range(VL):
        replicate_group(opa_scr, q)

    def step_pair(ip, carry):
        one_step(2 * ip, opa_scr, opb_scr)
        one_step(2 * ip + 1, opb_scr, opa_scr)
        return carry

    lax.fori_loop(0, tb // 2, step_pair, 0)

    @pl.when(tblk == nt - 1)
    def _():
        sT_ref[...] = S_scr[...]


def _rwkv_scan(kk, dec, b, kd, r, v, s0, G, L, H, n):
    DR = H * n
    NT = DR // LANES
    VL = LANES // H
    tb = _tile(L, 16, 2)
    nt = L // tb
    src = np.arange(3 * LANES)[None, :, None] % LANES
    dst = np.arange(LANES)[None, None, :]
    e = jnp.asarray(src == np.arange(VL)[:, None, None] * H + dst % H, BF16)
    nrow = N_KEY_OPS * NT * SUBLANES
    tmap = lambda d, g, t: g * nt + t + d * (nt - 1 - 2 * t)
    seq = pl.BlockSpec((tb * SUBLANES, DR), lambda d, g, t: (tmap(d, g, t), 0))
    seq_d = pl.BlockSpec((None, tb * SUBLANES, DR), lambda d, g, t: (d, tmap(d, g, t), 0))
    st = pl.BlockSpec((None, None, NT, n, SUBLANES, LANES), lambda d, g, t: (d, g, 0, 0, 0, 0))
    op_buf = pltpu.VMEM((N_KEY_OPS, n, SUBLANES, LANES), F32)
    return pl.pallas_call(
        functools.partial(_scan_kernel, tb=tb, nt=nt, n=n, H=H),
        grid=(2, G, nt),
        in_specs=[seq, seq_d, seq_d, seq_d, seq, seq,
                  pl.BlockSpec((VL, 3 * LANES, LANES), lambda d, g, t: (0, 0, 0)), st],
        out_specs=[seq_d, st],
        out_shape=[jax.ShapeDtypeStruct((2, G * L * SUBLANES, DR), F32),
                   jax.ShapeDtypeStruct(s0.shape, F32)],
        scratch_shapes=[pltpu.VMEM((NT, n, SUBLANES, LANES), F32), op_buf, op_buf,
                        pltpu.VMEM((nrow, LANES), F32), pltpu.VMEM((nrow, 3 * LANES), BF16),
                        pltpu.VMEM((2, NT, SUBLANES, LANES), F32)],
        compiler_params=_cparams(3),
        name="rwkv_scan",
    )(kk, dec, b, kd, r, v, e, s0)


def _gate_kernel(yf_ref, yb_ref, bonus_ref, g_ref, lw_ref, lb_ref, o_ref, *, H, n):
    tt, _, DR = yf_ref.shape
    lnw = lw_ref[...]
    lnb = lb_ref[...]

    def groupnorm(y):
        yc = y - _head_sum(y, H) * (1.0 / n)
        var = _head_sum(yc * yc, H) * (1.0 / n)
        return yc * lax.rsqrt(var + LNX_EPS) * lnw + lnb

    flat = lambda ref: ref[...].reshape(tt * SUBLANES, DR)
    o = (groupnorm(flat(yf_ref)) + groupnorm(flat(yb_ref)) + flat(bonus_ref)) * flat(g_ref)
    o_ref[...] = jnp.swapaxes(o.reshape(tt, SUBLANES, DR), 0, 1).astype(o_ref.dtype)


def _rwkv_gate(y_dirs, bonus, g, lnw, lnb, G, L, H, n):
    DR = g.shape[1]
    tt = _tile(L, 32, 16)
    nt = L // tt
    t3 = lambda a: a.reshape(a.shape[:-2] + (G * L, SUBLANES, DR))
    dspec = lambda d: pl.BlockSpec((None, tt, SUBLANES, DR), lambda gi, i: (d, gi * nt + i, 0, 0))
    mspec = pl.BlockSpec((tt, SUBLANES, DR), lambda gi, i: (gi * nt + i, 0, 0))
    wspec = pl.BlockSpec((1, DR), lambda gi, i: (0, 0))
    out = pl.pallas_call(
        functools.partial(_gate_kernel, H=H, n=n),
        grid=(G, nt),
        in_specs=[dspec(0), dspec(1), mspec, mspec, wspec, wspec],
        out_specs=pl.BlockSpec((SUBLANES, tt, DR), lambda gi, i: (gi, i, 0)),
        out_shape=jax.ShapeDtypeStruct((G * SUBLANES, L, DR), BF16),
        compiler_params=_cparams(2),
        name="rwkv_gate",
    )(t3(y_dirs), t3(y_dirs), t3(bonus), t3(g), lnw.reshape(1, DR), lnb.reshape(1, DR))
    return out.reshape(G * SUBLANES * L, DR)


def _mm_tmajor_kernel(x_ref, w_ref, o_ref, *, tt):
    x = x_ref[...]
    y = jnp.dot(x.reshape(SUBLANES * tt, x.shape[-1]), w_ref[...], preferred_element_type=F32)
    o_ref[...] = jnp.swapaxes(y.reshape(SUBLANES, tt, y.shape[-1]), 0, 1)


def _matmul_time_major(x, w, B, L):
    K = x.shape[1]
    N = w.shape[1]
    G = B // SUBLANES
    tt = _tile(L, 128, 16)
    tn = _tile(N, 512)
    nt = L // tt
    return pl.pallas_call(
        functools.partial(_mm_tmajor_kernel, tt=tt),
        grid=(G, nt, N // tn),
        in_specs=[pl.BlockSpec((SUBLANES, tt, K), lambda g, i, j: (g, i, 0)),
                  pl.BlockSpec((K, tn), lambda g, i, j: (0, j))],
        out_specs=pl.BlockSpec((tt, SUBLANES, tn), lambda g, i, j: (g * nt + i, 0, j)),
        out_shape=jax.ShapeDtypeStruct((G * L, SUBLANES, N), F32),
        compiler_params=_cparams(3),
        name="proj_rwkv",
    )(x.reshape(B, L, K), w).reshape(B * L, N)


def _head_minor(a, H, n):
    return a.reshape(a.shape[:-1] + (H, n)).swapaxes(-1, -2).reshape(a.shape)


def _rwkv_branch(h, w_r, B, L, H, n, s_f0, s_b0, rw):
    assert B % SUBLANES == 0 and LANES % H == 0 and n % (LANES // H) == 0, (B, H, n)
    DR = H * n
    G = B // SUBLANES
    VL = LANES // H
    NT = n // VL
    pr = _matmul_time_major(h, w_r, B, L)
    r, v, kk, dec, b, kd, g, bonus = _rwkv_pre(pr, L, rw, DR, H)
    s0 = jnp.stack([s_f0, s_b0]).astype(F32).reshape(2, G, SUBLANES, H, NT, VL, n)
    s0 = jnp.transpose(s0, (0, 1, 4, 6, 2, 5, 3)).reshape(2, G, NT, n, SUBLANES, LANES)
    y_dirs, sT = _rwkv_scan(kk, dec, b, kd, r, v, s0, G, L, H, n)
    o_rwkv = _rwkv_gate(y_dirs, bonus, g, rw["lnx_w"], rw["lnx_b"], G, L, H, n)
    sT = jnp.transpose(sT.reshape(2, G, NT, n, SUBLANES, VL, H), (0, 1, 4, 6, 2, 5, 3))
    sT = sT.reshape(2, B, H, n, n)
    return o_rwkv, sT[0], sT[1]


def _pad_rows(w, rows):
    return jnp.pad(w, [(0, 0)] * (w.ndim - 2) + [(0, rows - w.shape[-2]), (0, 0)])


def kernel(x_prompt, x_sample, cache_attn_k, cache_attn_v, state_rwkv_fwd, state_rwkv_bwd, c, c_ctx,
           w_mod, b_mod, norm1_w, norm2_w, norm_f_w, w_in, tshift_mu, attn_rpb,
           rwkv_w0, rwkv_w2, rwkv_a0, rwkv_a2, rwkv_g2, rwkv_k_k, rwkv_k_a, rwkv_r_k, lnx_w, lnx_b,
           w_o_attn, w_o_rwkv, w_out, w_mlp1, w_mlp2):
    Bc, Lc, D = x_prompt.shape
    Bs, Ls, _ = x_sample.shape
    depth = w_mod.shape[0]
    HA, dh = cache_attn_k.shape[3], cache_attn_k.shape[4]
    HR, n = state_rwkv_fwd.shape[2], state_rwkv_fwd.shape[3]
    DA, DR = HA * dh, HR * n
    lw, la, lg = rwkv_w2.shape[2], rwkv_a2.shape[2], rwkv_g2.shape[1]
    LW, LA, LG = _round_up(lw, LANES), _round_up(la, LANES), _round_up(lg, LANES)
    past = cache_attn_k.shape[2]
    rows = Ls // GRID_W

    xc = x_prompt.reshape(Bc * Lc, D)
    xs = x_sample.reshape(Bs * Ls, D)
    cond_rows = _round_up(Bs + 1, SUBLANES)
    cond = jnp.zeros((cond_rows, D), F32).at[:Bs].set(c).at[Bs].set(c_ctx)
    pad_c = lambda a, wdt: jnp.pad(a, [(0, 0)] * (a.ndim - 1) + [(0, wdt - a.shape[-1])])

    new_k, new_v, new_sf, new_sb = [], [], [], []
    for l in range(depth):
        mod3 = _adaln(cond, w_mod[l], b_mod[l]).reshape(cond_rows * 6, 1, D)

        wi = w_in[l]
        o1, o2 = 3 * DA, 3 * DA + 3 * DR
        w_a = wi[:, :o1].astype(BF16)
        hm = functools.partial(_head_minor, H=HR, n=n)
        rkv = wi[:, o1:o2].reshape(D, 3, DR)
        w_r = jnp.concatenate([hm(rkv).reshape(D, 3 * DR), pad_c(wi[:, o2:o2 + lw], LW),
                               pad_c(wi[:, o2 + lw:o2 + lw + la], LA),
                               pad_c(wi[:, o2 + lw + la:o2 + lw + la + lg], LG)], axis=1).astype(BF16)
        w_g = wi[:, o2 + lw + la + lg:].astype(BF16)
        mu = tshift_mu[l]
        mu_r = jnp.concatenate([hm(mu[:3 * DR].reshape(3, DR)).reshape(-1), pad_c(mu[3 * DR:3 * DR + lw], LW),
                                pad_c(mu[3 * DR + lw:3 * DR + lw + la], LA),
                                pad_c(mu[3 * DR + lw + la:], LG)]).reshape(1, -1)
        rw = dict(mu=mu_r, LW=LW, LA=LA,
                  w0=hm(rwkv_w0[l]).reshape(2, 1, DR), w2=_pad_rows(hm(rwkv_w2[l]), LW).astype(BF16),
                  a0=hm(rwkv_a0[l]).reshape(2, 1, DR), a2=_pad_rows(hm(rwkv_a2[l]), LA).astype(BF16),
                  g2=_pad_rows(hm(rwkv_g2[l]), LG).astype(BF16),
                  k_k=hm(rwkv_k_k[l]), k_a=hm(rwkv_k_a[l]), r_k=hm(rwkv_r_k[l].reshape(-1)),
                  lnx_w=hm(lnx_w[l]), lnx_b=hm(lnx_b[l]))
        w_oa = w_o_attn[l].astype(BF16)
        w_or = hm(w_o_rwkv[l].T).T.astype(BF16)
        w_o, w_1, w_2 = w_out[l].astype(BF16), w_mlp1[l].astype(BF16), w_mlp2[l].astype(BF16)
        bias_tab = _nbr_bias_table(attn_rpb[l], rows)

        def block(x, B, L, rows_per_cond, cond_base, attention, s_f0, s_b0, pa_dtype):
            h = _norm_modulate(x, norm1_w[l], mod3, (0, 1), rows_per_cond, cond_base)
            pa = _matmul(h, w_a, pa_dtype, name="proj_attn")
            pg = _matmul(h, w_g, F32, name="proj_gate")
            o_attn = attention(pa)
            o_rwkv, s_f, s_b = _rwkv_branch(h, w_r, B, L, HR, n, s_f0, s_b0, rw)
            merged = _merge(o_attn, w_oa, o_rwkv, w_or, pg)
            x = _matmul_gated_residual(merged, w_o, x, mod3, 2, rows_per_cond, cond_base, "out_proj")
            h = _norm_modulate(x, norm2_w[l], mod3, (3, 4), rows_per_cond, cond_base)
            a = _matmul(h, w_1, BF16, relu2=True, name="mlp_up")
            x = _matmul_gated_residual(a, w_2, x, mod3, 5, rows_per_cond, cond_base, "mlp_down")
            return x, pa, s_f, s_b

        z = jnp.zeros((Bc, HR, n, n), F32)
        xc, pa_c, s_f, s_b = block(
            xc, Bc, Lc, Bc * Lc, Bs,
            lambda pa: _context_attention(pa, Bc, Lc, HA, dh), z, z, F32)
        new_k.append(pa_c[:, DA:2 * DA].reshape(Bc, Lc, HA, dh))
        new_v.append(pa_c[:, 2 * DA:].reshape(Bc, Lc, HA, dh))
        new_sf.append(s_f)
        new_sb.append(s_b)

        kc = cache_attn_k[:, l].reshape(Bs * past, DA)
        vc = cache_attn_v[:, l].reshape(Bs * past, DA)
        xs, _, _, _ = block(
            xs, Bs, Ls, Ls, 0,
            lambda pa: _neighbourhood_attention(pa, kc, vc, bias_tab, Bs, Ls, HA, dh),
            state_rwkv_fwd[:, l], state_rwkv_bwd[:, l], BF16)

    y_prompt = _final_norm(xc, norm_f_w).reshape(Bc, Lc, D)
    y_sample = _final_norm(xs, norm_f_w).reshape(Bs, Ls, D)
    return (y_prompt, y_sample, jnp.stack(new_k, axis=1), jnp.stack(new_v, axis=1),
            jnp.stack(new_sf, axis=1), jnp.stack(new_sb, axis=1))
```

```python
import functools

import numpy as np
import jax
import jax.numpy as jnp
from jax import lax
from jax.experimental import pallas as pl
from jax.experimental.pallas import tpu as pltpu

GRID_W = 64
WIN_ROWS = 8
WIN_COLS = 16
RMS_EPS = 1e-6
LNX_EPS = 64e-5

LANES = 128
SUBLANES = 8
VMEM_LIMIT_BYTES = 56 * 1024 * 1024

F32 = jnp.float32
BF16 = jnp.bfloat16
NEG_BIG = -1e30


def _cparams(n_axes):
    return pltpu.CompilerParams(dimension_semantics=("arbitrary",) * n_axes,
                                vmem_limit_bytes=VMEM_LIMIT_BYTES)


def _tile(n, pref, unit=LANES):
    if n <= pref:
        return n
    t = (pref // unit) * unit
    while t > unit and n % t:
        t -= unit
    assert n % t == 0, (n, pref, unit)
    return t


def _round_up(n, m):
    return (n + m - 1) // m * m


def _sigmoid(x):
    return 1.0 / (1.0 + jnp.exp(-x))


def _softplus(x):
    return jnp.maximum(x, 0.0) + jnp.log(1.0 + jnp.exp(-jnp.abs(x)))


def _mod_kernel(c_ref, w_ref, b_ref, o_ref):
    c = c_ref[...]
    s = (c * _sigmoid(c)).astype(BF16)
    o_ref[...] = jnp.dot(s, w_ref[...].astype(BF16), preferred_element_type=F32) + b_ref[...]


def _adaln(cond, w_mod, b_mod):
    R, D = cond.shape
    N = w_mod.shape[1]
    tn = _tile(N, 512)
    return pl.pallas_call(
        _mod_kernel,
        grid=(N // tn,),
        in_specs=[pl.BlockSpec((R, D), lambda j: (0, 0)),
                  pl.BlockSpec((D, tn), lambda j: (0, j)),
                  pl.BlockSpec((1, tn), lambda j: (0, j))],
        out_specs=pl.BlockSpec((R, tn), lambda j: (0, j)),
        out_shape=jax.ShapeDtypeStruct((R, N), F32),
        compiler_params=_cparams(1),
        name="adaln",
    )(cond, w_mod, b_mod.reshape(1, N))


def _norm_mod_kernel(x_ref, w_ref, sh_ref, sc_ref, o_ref):
    x = x_ref[...]
    y = x * lax.rsqrt(jnp.mean(x * x, axis=-1, keepdims=True) + RMS_EPS) * w_ref[...]
    o_ref[...] = (y * (1.0 + sc_ref[...]) + sh_ref[...]).astype(o_ref.dtype)


def _norm_kernel(x_ref, w_ref, o_ref):
    x = x_ref[...]
    y = x * lax.rsqrt(jnp.mean(x * x, axis=-1, keepdims=True) + RMS_EPS) * w_ref[...]
    o_ref[...] = y.astype(o_ref.dtype)


def _mod_spec(part, tm, rows_per_cond, cond_base, D):
    return pl.BlockSpec((None, 1, D),
                        lambda i, *_: ((cond_base + (i * tm) // rows_per_cond) * 6 + part, 0, 0))


def _norm_modulate(x, w, mod3, parts, rows_per_cond, cond_base):
    T, D = x.shape
    tm = _tile(min(T, rows_per_cond), 256, SUBLANES)
    return pl.pallas_call(
        _norm_mod_kernel,
        grid=(T // tm,),
        in_specs=[pl.BlockSpec((tm, D), lambda i: (i, 0)),
                  pl.BlockSpec((1, D), lambda i: (0, 0)),
                  _mod_spec(parts[0], tm, rows_per_cond, cond_base, D),
                  _mod_spec(parts[1], tm, rows_per_cond, cond_base, D)],
        out_specs=pl.BlockSpec((tm, D), lambda i: (i, 0)),
        out_shape=jax.ShapeDtypeStruct((T, D), BF16),
        compiler_params=_cparams(1),
        name="norm_modulate",
    )(x, w.reshape(1, D), mod3, mod3)


def _final_norm(x, w):
    T, D = x.shape
    tm = _tile(T, 256, SUBLANES)
    return pl.pallas_call(
        _norm_kernel,
        grid=(T // tm,),
        in_specs=[pl.BlockSpec((tm, D), lambda i: (i, 0)),
                  pl.BlockSpec((1, D), lambda i: (0, 0))],
        out_specs=pl.BlockSpec((tm, D), lambda i: (i, 0)),
        out_shape=jax.ShapeDtypeStruct((T, D), F32),
        compiler_params=_cparams(1),
        name="final_norm",
    )(x, w.reshape(1, D))


def _mm_kernel(x_ref, w_ref, o_ref):
    o_ref[...] = jnp.dot(x_ref[...], w_ref[...], preferred_element_type=F32).astype(o_ref.dtype)


def _mm_relu2_kernel(x_ref, w_ref, o_ref):
    a = jnp.maximum(jnp.dot(x_ref[...], w_ref[...], preferred_element_type=F32), 0.0)
    o_ref[...] = (a * a).astype(o_ref.dtype)


def _matmul(x, w, out_dtype, *, relu2=False, tm_pref=1024, tn_pref=1024, name="matmul"):
    M, K = x.shape
    N = w.shape[1]
    tm, tn = _tile(M, tm_pref, SUBLANES), _tile(N, tn_pref)
    return pl.pallas_call(
        _mm_relu2_kernel if relu2 else _mm_kernel,
        grid=(M // tm, N // tn),
        in_specs=[pl.BlockSpec((tm, K), lambda i, j: (i, 0)),
                  pl.BlockSpec((K, tn), lambda i, j: (0, j))],
        out_specs=pl.BlockSpec((tm, tn), lambda i, j: (i, j)),
        out_shape=jax.ShapeDtypeStruct((M, N), out_dtype),
        compiler_params=_cparams(2),
        name=name,
    )(x, w)


def _merge_kernel(oa_ref, wa_ref, or_ref, wr_ref, ga_ref, gr_ref, o_ref):
    ya = jnp.dot(oa_ref[...], wa_ref[...], preferred_element_type=F32)
    yr = jnp.dot(or_ref[...], wr_ref[...], preferred_element_type=F32)
    o_ref[...] = (_sigmoid(ga_ref[...]) * ya + _sigmoid(gr_ref[...]) * yr).astype(o_ref.dtype)


def _merge(o_attn, w_oa, o_rwkv, w_or, pg):
    T, DA = o_attn.shape
    DR = o_rwkv.shape[1]
    D = w_oa.shape[1]
    tm, tn = _tile(T, 512, SUBLANES), _tile(D, 1024)
    nj = D // tn
    return pl.pallas_call(
        _merge_kernel,
        grid=(T // tm, nj),
        in_specs=[pl.BlockSpec((tm, DA), lambda i, j: (i, 0)),
                  pl.BlockSpec((DA, tn), lambda i, j: (0, j)),
                  pl.BlockSpec((tm, DR), lambda i, j: (i, 0)),
                  pl.BlockSpec((DR, tn), lambda i, j: (0, j)),
                  pl.BlockSpec((tm, tn), lambda i, j: (i, j)),
                  pl.BlockSpec((tm, tn), lambda i, j: (i, j + nj))],
        out_specs=pl.BlockSpec((tm, tn), lambda i, j: (i, j)),
        out_shape=jax.ShapeDtypeStruct((T, D), BF16),
        compiler_params=_cparams(2),
        name="merge",
    )(o_attn, w_oa, o_rwkv, w_or, pg, pg)


def _mm_resid_kernel(x_ref, w_ref, res_ref, gate_ref, o_ref, acc_ref, *, nk):
    k = pl.program_id(2)
    part = jnp.dot(x_ref[...], w_ref[...], preferred_element_type=F32)

    @pl.when(k == 0)
    def _():
        acc_ref[...] = part

    @pl.when(k > 0)
    def _():
        acc_ref[...] += part

    @pl.when(k == nk - 1)
    def _():
        o_ref[...] = res_ref[...] + gate_ref[...] * acc_ref[...]


def _matmul_gated_residual(x, w, res, mod3, part, rows_per_cond, cond_base, name):
    M, K = x.shape
    N = w.shape[1]
    tm = _tile(min(M, rows_per_cond), 1024, SUBLANES)
    tn, tk = _tile(N, 1024), _tile(K, 2048)
    nk = K // tk
    gate_spec = pl.BlockSpec(
        (None, 1, tn), lambda i, j, k: ((cond_base + (i * tm) // rows_per_cond) * 6 + part, 0, j))
    return pl.pallas_call(
        functools.partial(_mm_resid_kernel, nk=nk),
        grid=(M // tm, N // tn, nk),
        in_specs=[pl.BlockSpec((tm, tk), lambda i, j, k: (i, k)),
                  pl.BlockSpec((tk, tn), lambda i, j, k: (k, j)),
                  pl.BlockSpec((tm, tn), lambda i, j, k: (i, j)),
                  gate_spec],
        out_specs=pl.BlockSpec((tm, tn), lambda i, j, k: (i, j)),
        out_shape=jax.ShapeDtypeStruct((M, N), F32),
        scratch_shapes=[pltpu.VMEM((tm, tn), F32)],
        compiler_params=_cparams(3),
        name=name,
    )(x, w, res, mod3)


def _ctx_attn_kernel(q_ref, k_ref, v_ref, o_ref, *, scale):
    q = q_ref[...].astype(BF16)
    k = k_ref[...].astype(BF16)
    s = lax.dot_general(q, k, (((1,), (1,)), ((), ())), preferred_element_type=F32) * scale
    m = jnp.max(s, axis=-1, keepdims=True)
    e = jnp.exp(s - m)
    p = (e / jnp.sum(e, axis=-1, keepdims=True)).astype(BF16)
    o_ref[...] = jnp.dot(p, v_ref[...].astype(BF16), preferred_element_type=F32).astype(o_ref.dtype)


def _context_attention(pa, B, L, H, dh):
    spec = lambda off: pl.BlockSpec((L, dh), lambda b, h: (b, off + h))
    return pl.pallas_call(
        functools.partial(_ctx_attn_kernel, scale=dh ** -0.5),
        grid=(B, H),
        in_specs=[spec(0), spec(H), spec(2 * H)],
        out_specs=pl.BlockSpec((L, dh), lambda b, h: (b, h)),
        out_shape=jax.ShapeDtypeStruct((B * L, H * dh), BF16),
        compiler_params=_cparams(2),
        name="context_attention",
    )(pa, pa, pa)


def _nbr_bias_table(rpb, rows):
    kh = min(WIN_ROWS, rows)
    qc = np.arange(GRID_W)[:, None]
    kc = np.arange(GRID_W)[None, :]
    cs = np.clip(qc - WIN_COLS // 2, 0, GRID_W - WIN_COLS)
    col_ok = (kc >= cs) & (kc < cs + WIN_COLS)
    dc_idx = np.clip(kc - qc + WIN_COLS - 1, 0, 2 * WIN_COLS - 2)
    n_pat = 2 * WIN_ROWS - kh
    dr_idx = np.arange(n_pat)[:, None] + np.arange(kh)[None, :]
    tab = rpb[:, dr_idx][:, :, :, dc_idx]
    tab = jnp.where(col_ok[None, None, None], tab.astype(F32), NEG_BIG)
    tab = tab.transpose(1, 0, 3, 2, 4)
    return tab.reshape(n_pat, rpb.shape[0], GRID_W, kh * GRID_W)


def _nbr_attn_kernel(q_ref, k_ref, v_ref, kc_ref, vc_ref, bias_ref, o_ref, *, rows, kh, scale):
    kc = kc_ref[...].astype(BF16)
    vc = vc_ref[...].astype(BF16)
    nband = kh * GRID_W

    nr = 4 if rows % 4 == 0 else 1
    dn_t = (((1,), (1,)), ((), ()))

    def row_group(g, carry):
        q0 = pl.multiple_of(g * (nr * GRID_W), nr * GRID_W)
        q = q_ref[pl.ds(q0, nr * GRID_W), :].astype(BF16)
        s_ctx = lax.dot_general(q, kc, dn_t, preferred_element_type=F32) * scale
        s_loc, vbs = [], []
        for j in range(nr):
            r = g * nr + j
            rs = jnp.clip(r - WIN_ROWS // 2, 0, rows - kh)
            k0 = pl.multiple_of(rs * GRID_W, GRID_W)
            kb = k_ref[pl.ds(k0, nband), :].astype(BF16)
            vbs.append(v_ref[pl.ds(k0, nband), :].astype(BF16))
            s = lax.dot_general(q[j * GRID_W:(j + 1) * GRID_W], kb, dn_t, preferred_element_type=F32)
            s_loc.append(s * scale + bias_ref[rs - r + WIN_ROWS - 1])
        m_ctx = jnp.max(s_ctx, axis=-1, keepdims=True)
        p_loc, p_ctx = [], []
        for j in range(nr):
            sl = slice(j * GRID_W, (j + 1) * GRID_W)
            m = jnp.maximum(jnp.max(s_loc[j], axis=-1, keepdims=True), m_ctx[sl])
            e_loc = jnp.exp(s_loc[j] - m)
            e_ctx = jnp.exp(s_ctx[sl] - m)
            den = jnp.sum(e_loc, axis=-1, keepdims=True) + jnp.sum(e_ctx, axis=-1, keepdims=True)
            p_loc.append((e_loc / den).astype(BF16))
            p_ctx.append((e_ctx / den).astype(BF16))
        o_ctx = jnp.dot(jnp.concatenate(p_ctx, axis=0), vc, preferred_element_type=F32)
        for j in range(nr):
            o = jnp.dot(p_loc[j], vbs[j], preferred_element_type=F32) + o_ctx[j * GRID_W:(j + 1) * GRID_W]
            o_ref[pl.ds(q0 + j * GRID_W, GRID_W), :] = o.astype(o_ref.dtype)
        return carry

    lax.fori_loop(0, rows // nr, row_group, 0)


def _neighbourhood_attention(pa, k_ctx, v_ctx, bias_tab, B, N, H, dh):
    rows = N // GRID_W
    kh = min(WIN_ROWS, rows)
    Lc = k_ctx.shape[0] // B
    n_pat = bias_tab.shape[0]
    spec = lambda off: pl.BlockSpec((N, dh), lambda b, h: (b, off + h))
    cspec = pl.BlockSpec((Lc, dh), lambda b, h: (b, h))
    return pl.pallas_call(
        functools.partial(_nbr_attn_kernel, rows=rows, kh=kh, scale=dh ** -0.5),
        grid=(B, H),
        in_specs=[spec(0), spec(H), spec(2 * H), cspec, cspec,
                  pl.BlockSpec((n_pat, None, GRID_W, kh * GRID_W), lambda b, h: (0, h, 0, 0))],
        out_specs=pl.BlockSpec((N, dh), lambda b, h: (b, h)),
        out_shape=jax.ShapeDtypeStruct((B * N, H * dh), BF16),
        compiler_params=_cparams(2),
        name="neighbourhood_attention",
    )(pa, pa, pa, k_ctx, v_ctx, bias_tab)


def _head_sum(x, H):
    nt = x.shape[1] // LANES
    s = x[:, 0:LANES]
    for j in range(1, nt):
        s = s + x[:, j * LANES:(j + 1) * LANES]
    span = LANES // 2
    while span >= H:
        s = s + pltpu.roll(s, span, 1)
        span //= 2
    return jnp.concatenate([s] * nt, axis=1)


def _rwkv_pre_kernel(p_ref, prev_ref, next_ref, mu_ref, w0_ref, w2_ref, a0_ref, a2_ref, g2_ref,
                     kkw_ref, kaw_ref, rkw_ref,
                     r_ref, v_ref, kk_ref, dec_ref, b_ref, kd_ref, g_ref, bonus_ref,
                     *, tiles_per_seq, DR, LW, LA, H):
    i = pl.program_id(0)
    p = p_ref[...]
    first = (i % tiles_per_seq) == 0
    last = (i % tiles_per_seq) == tiles_per_seq - 1
    halo_prev = jnp.where(first, 0.0, prev_ref[...])
    halo_next = jnp.where(last, 0.0, next_ref[...])
    prev = jnp.concatenate([halo_prev, p[:-SUBLANES]], axis=0)
    nxt = jnp.concatenate([p[SUBLANES:], halo_next], axis=0)
    xr = p + mu_ref[...] * (0.5 * (prev + nxt) - p)
    r = xr[:, 0:DR]
    k = xr[:, DR:2 * DR]
    v = xr[:, 2 * DR:3 * DR]
    o = 3 * DR
    tw = jnp.tanh(xr[:, o:o + LW]).astype(BF16)
    xa = xr[:, o + LW:o + LW + LA].astype(BF16)
    sg = _sigmoid(xr[:, o + LW + LA:]).astype(BF16)
    kk = k * kkw_ref[...]
    kk = kk / jnp.maximum(jnp.sqrt(_head_sum(kk * kk, H)), 1e-12)
    r_ref[...] = r
    v_ref[...] = v
    kk_ref[...] = kk
    g_ref[...] = jnp.dot(sg, g2_ref[...], preferred_element_type=F32)
    rk = r * rkw_ref[...]
    bonus = None
    for d in range(2):
        wl = w0_ref[d] + jnp.dot(tw, w2_ref[d], preferred_element_type=F32)
        wl = -_softplus(-wl) - 0.5
        dec_ref[d] = jnp.exp(-jnp.exp(wl))
        a = _sigmoid(a0_ref[d] + jnp.dot(xa, a2_ref[d], preferred_element_type=F32))
        kd = k * (1.0 + (a - 1.0) * kaw_ref[...])
        kd_ref[d] = kd
        b_ref[d] = kk * a
        s = _head_sum(rk * kd, H)
        bonus = s if bonus is None else bonus + s
    bonus_ref[...] = bonus * v


def _rwkv_pre(pr, L, rw, DR, H):
    T, W = pr.shape
    hb = _tile(L, 16, 1)
    tl = hb * SUBLANES
    tiles_per_seq = L // hb
    nhb = T // SUBLANES
    LW, LA = rw["LW"], rw["LA"]
    LG = rw["g2"].shape[0]
    full = lambda shape: pl.BlockSpec(shape, lambda i: (0,) * len(shape))
    out_main = pl.BlockSpec((tl, DR), lambda i: (i, 0))
    out_dir = pl.BlockSpec((2, tl, DR), lambda i: (0, i, 0))
    sd_main = jax.ShapeDtypeStruct((T, DR), F32)
    sd_dir = jax.ShapeDtypeStruct((2, T, DR), F32)
    row = lambda w: w.reshape(1, DR)
    return pl.pallas_call(
        functools.partial(_rwkv_pre_kernel, tiles_per_seq=tiles_per_seq, DR=DR, LW=LW, LA=LA, H=H),
        grid=(T // tl,),
        in_specs=[pl.BlockSpec((tl, W), lambda i: (i, 0)),
                  pl.BlockSpec((SUBLANES, W), lambda i: (jnp.maximum(i * hb - 1, 0), 0)),
                  pl.BlockSpec((SUBLANES, W), lambda i: (jnp.minimum((i + 1) * hb, nhb - 1), 0)),
                  full((1, W)), full((2, 1, DR)), full((2, LW, DR)), full((2, 1, DR)),
                  full((2, LA, DR)), full((LG, DR)), full((1, DR)), full((1, DR)), full((1, DR))],
        out_specs=[out_main, out_main, out_main, out_dir, out_dir, out_dir, out_main, out_main],
        out_shape=[sd_main, sd_main, sd_main, sd_dir, sd_dir, sd_dir, sd_main, sd_main],
        compiler_params=_cparams(1),
        name="rwkv_pre",
    )(pr, pr, pr, rw["mu"], rw["w0"], rw["w2"], rw["a0"], rw["a2"], rw["g2"],
      row(rw["k_k"]), row(rw["k_a"]), row(rw["r_k"]))


N_KEY_OPS = 5


def _scan_value_tiles(nt):
    return min(4, nt)


def _scan_kernel(kk_ref, dec_ref, b_ref, kd_ref, r_ref, v_ref, e_ref, s0_ref, y_ref, sT_ref,
                 S_scr, opa_scr, opb_scr, nat_scr, lhs_scr, vy_scr, *, tb, nt, n, H):
    VL = LANES // H
    NT = n // VL
    VB = _scan_value_tiles(NT)
    NVG = NT // VB
    QPI = VL // NVG
    d = pl.program_id(0)
    tblk = pl.program_id(2)
    key_refs = (kk_ref, dec_ref, b_ref, kd_ref, r_ref)

    @pl.when(tblk == 0)
    def _():
        S_scr[...] = s0_ref[...]

    def block_rows(i):
        t = i + d * (tb - 1 - 2 * i)
        return pl.ds(pl.multiple_of(t * SUBLANES, SUBLANES), SUBLANES)

    def stage(i):
        rows = block_rows(i)
        for oi, ref in enumerate(key_refs):
            x = ref[rows, :]
            for j in range(NT):
                nat_scr[pl.ds((oi * NT + j) * SUBLANES, SUBLANES), :] = x[:, j * LANES:(j + 1) * LANES]
        x = nat_scr[...]
        x1 = x.astype(BF16)
        r1 = x - x1.astype(F32)
        x2 = r1.astype(BF16)
        x3 = (r1 - x2.astype(F32)).astype(BF16)
        lhs_scr[...] = jnp.concatenate([x1, x2, x3], axis=1)

    def replicate_group(op_dst, q):
        rep = jnp.dot(lhs_scr[...], e_ref[q], preferred_element_type=F32)
        for oi in range(N_KEY_OPS):
            for j in range(NT):
                r0 = (oi * NT + j) * SUBLANES
                op_dst[j * VL + q, oi] = rep[r0:r0 + SUBLANES]

    def one_step(i, op_cur, op_nxt):
        rows = block_rows(i)
        v_t = v_ref[rows, :]
        for j in range(NT):
            vy_scr[0, j] = v_t[:, j * LANES:(j + 1) * LANES]
        stage(jnp.minimum(i + 1, tb - 1))

        def value_group(vg, c):
            for qq in range(QPI):
                replicate_group(op_nxt, vg * QPI + qq)
            vts = [vg * VB + u for u in range(VB)]
            vals = [vy_scr[0, vt] for vt in vts]
            acc = [[None, None] for _ in vts]
            for k in range(n):
                kk_k = op_cur[k, 0]
                for u, vt in enumerate(vts):
                    p = S_scr[vg, k, u] * kk_k
                    acc[u][k % 2] = p if acc[u][k % 2] is None else acc[u][k % 2] + p
            sa = [-(a0 + a1) for a0, a1 in acc]
            acc = [[None, None] for _ in vts]
            for k in range(n):
                w_k, b_k, kd_k, r_kk = op_cur[k, 1], op_cur[k, 2], op_cur[k, 3], op_cur[k, 4]
                for u, vt in enumerate(vts):
                    s_new = S_scr[vg, k, u] * w_k + sa[u] * b_k + vals[u] * kd_k
                    S_scr[vg, k, u] = s_new
                    p = s_new * r_kk
                    acc[u][k % 2] = p if acc[u][k % 2] is None else acc[u][k % 2] + p
            for u, vt in enumerate(vts):
                vy_scr[1, vt] = acc[u][0] + acc[u][1]
            return c

        lax.fori_loop(0, NVG, value_group, 0)
        y_ref[rows, :] = jnp.concatenate([vy_scr[1, j] for j in range(NT)], axis=1)

    stage(0)
    for q in range(VL):
        replicate_group(opa_scr, q)

    def step_pair(ip, carry):
        one_step(2 * ip, opa_scr, opb_scr)
        one_step(2 * ip + 1, opb_scr, opa_scr)
        return carry

    lax.fori_loop(0, tb // 2, step_pair, 0)

    @pl.when(tblk == nt - 1)
    def _():
        sT_ref[...] = S_scr[...]


def _rwkv_scan(kk, dec, b, kd, r, v, s0, G, L, H, n):
    DR = H * n
    NT = DR // LANES
    VL = LANES // H
    tb = _tile(L, 32, 2)
    nt = L // tb
    src = np.arange(3 * LANES)[None, :, None] % LANES
    dst = np.arange(LANES)[None, None, :]
    e = jnp.asarray(src == np.arange(VL)[:, None, None] * H + dst % H, BF16)
    nrow = N_KEY_OPS * NT * SUBLANES
    tmap = lambda d, g, t: g * nt + t + d * (nt - 1 - 2 * t)
    seq = pl.BlockSpec((tb * SUBLANES, DR), lambda d, g, t: (tmap(d, g, t), 0))
    seq_d = pl.BlockSpec((None, tb * SUBLANES, DR), lambda d, g, t: (d, tmap(d, g, t), 0))
    st = pl.BlockSpec((None, None) + s0.shape[2:], lambda d, g, t: (d, g, 0, 0, 0, 0, 0))
    op_buf = pltpu.VMEM((n, N_KEY_OPS, SUBLANES, LANES), F32)
    return pl.pallas_call(
        functools.partial(_scan_kernel, tb=tb, nt=nt, n=n, H=H),
        grid=(2, G, nt),
        in_specs=[seq, seq_d, seq_d, seq_d, seq, seq,
                  pl.BlockSpec((VL, 3 * LANES, LANES), lambda d, g, t: (0, 0, 0)), st],
        out_specs=[seq_d, st],
        out_shape=[jax.ShapeDtypeStruct((2, G * L * SUBLANES, DR), F32),
                   jax.ShapeDtypeStruct(s0.shape, F32)],
        scratch_shapes=[pltpu.VMEM(s0.shape[2:], F32), op_buf, op_buf,
                        pltpu.VMEM((nrow, LANES), F32), pltpu.VMEM((nrow, 3 * LANES), BF16),
                        pltpu.VMEM((2, NT, SUBLANES, LANES), F32)],
        compiler_params=_cparams(3),
        name="rwkv_scan",
    )(kk, dec, b, kd, r, v, e, s0)


def _gate_kernel(yf_ref, yb_ref, bonus_ref, g_ref, lw_ref, lb_ref, o_ref, *, H, n):
    tt, _, DR = yf_ref.shape
    lnw = lw_ref[...]
    lnb = lb_ref[...]

    def groupnorm(y):
        yc = y - _head_sum(y, H) * (1.0 / n)
        var = _head_sum(yc * yc, H) * (1.0 / n)
        return yc * lax.rsqrt(var + LNX_EPS) * lnw + lnb

    flat = lambda ref: ref[...].reshape(tt * SUBLANES, DR)
    o = (groupnorm(flat(yf_ref)) + groupnorm(flat(yb_ref)) + flat(bonus_ref)) * flat(g_ref)
    o_ref[...] = jnp.swapaxes(o.reshape(tt, SUBLANES, DR), 0, 1).astype(o_ref.dtype)


def _rwkv_gate(y_dirs, bonus, g, lnw, lnb, G, L, H, n):
    DR = g.shape[1]
    tt = _tile(L, 32, 16)
    nt = L // tt
    t3 = lambda a: a.reshape(a.shape[:-2] + (G * L, SUBLANES, DR))
    dspec = lambda d: pl.BlockSpec((None, tt, SUBLANES, DR), lambda gi, i: (d, gi * nt + i, 0, 0))
    mspec = pl.BlockSpec((tt, SUBLANES, DR), lambda gi, i: (gi * nt + i, 0, 0))
    wspec = pl.BlockSpec((1, DR), lambda gi, i: (0, 0))
    out = pl.pallas_call(
        functools.partial(_gate_kernel, H=H, n=n),
        grid=(G, nt),
        in_specs=[dspec(0), dspec(1), mspec, mspec, wspec, wspec],
        out_specs=pl.BlockSpec((SUBLANES, tt, DR), lambda gi, i: (gi, i, 0)),
        out_shape=jax.ShapeDtypeStruct((G * SUBLANES, L, DR), BF16),
        compiler_params=_cparams(2),
        name="rwkv_gate",
    )(t3(y_dirs), t3(y_dirs), t3(bonus), t3(g), lnw.reshape(1, DR), lnb.reshape(1, DR))
    return out.reshape(G * SUBLANES * L, DR)


def _mm_tmajor_kernel(x_ref, w_ref, o_ref, *, tt):
    x = x_ref[...]
    y = jnp.dot(x.reshape(SUBLANES * tt, x.shape[-1]), w_ref[...], preferred_element_type=F32)
    o_ref[...] = jnp.swapaxes(y.reshape(SUBLANES, tt, y.shape[-1]), 0, 1)


def _matmul_time_major(x, w, B, L):
    K = x.shape[1]
    N = w.shape[1]
    G = B // SUBLANES
    tt = _tile(L, 128, 16)
    tn = _tile(N, 512)
    nt = L // tt
    return pl.pallas_call(
        functools.partial(_mm_tmajor_kernel, tt=tt),
        grid=(G, nt, N // tn),
        in_specs=[pl.BlockSpec((SUBLANES, tt, K), lambda g, i, j: (g, i, 0)),
                  pl.BlockSpec((K, tn), lambda g, i, j: (0, j))],
        out_specs=pl.BlockSpec((tt, SUBLANES, tn), lambda g, i, j: (g * nt + i, 0, j)),
        out_shape=jax.ShapeDtypeStruct((G * L, SUBLANES, N), F32),
        compiler_params=_cparams(3),
        name="proj_rwkv",
    )(x.reshape(B, L, K), w).reshape(B * L, N)


def _head_minor(a, H, n):
    return a.reshape(a.shape[:-1] + (H, n)).swapaxes(-1, -2).reshape(a.shape)


def _rwkv_branch(h, w_r, B, L, H, n, s_f0, s_b0, rw):
    assert B % SUBLANES == 0 and LANES % H == 0 and n % (LANES // H) == 0, (B, H, n)
    DR = H * n
    G = B // SUBLANES
    VL = LANES // H
    NT = n // VL
    pr = _matmul_time_major(h, w_r, B, L)
    r, v, kk, dec, b, kd, g, bonus = _rwkv_pre(pr, L, rw, DR, H)
    s0 = jnp.stack([s_f0, s_b0]).astype(F32).reshape(2, G, SUBLANES, H, NT, VL, n)
    VB = _scan_value_tiles(NT)
    s0 = jnp.transpose(s0, (0, 1, 4, 6, 2, 5, 3)).reshape(2, G, NT // VB, VB, n, SUBLANES, LANES)
    s0 = jnp.swapaxes(s0, 3, 4)
    y_dirs, sT = _rwkv_scan(kk, dec, b, kd, r, v, s0, G, L, H, n)
    o_rwkv = _rwkv_gate(y_dirs, bonus, g, rw["lnx_w"], rw["lnx_b"], G, L, H, n)
    sT = jnp.swapaxes(sT, 3, 4).reshape(2, G, NT, n, SUBLANES, VL, H)
    sT = jnp.transpose(sT, (0, 1, 4, 6, 2, 5, 3)).reshape(2, B, H, n, n)
    return o_rwkv, sT[0], sT[1]


def _pad_rows(w, rows):
    return jnp.pad(w, [(0, 0)] * (w.ndim - 2) + [(0, rows - w.shape[-2]), (0, 0)])


def kernel(x_prompt, x_sample, cache_attn_k, cache_attn_v, state_rwkv_fwd, state_rwkv_bwd, c, c_ctx,
           w_mod, b_mod, norm1_w, norm2_w, norm_f_w, w_in, tshift_mu, attn_rpb,
           rwkv_w0, rwkv_w2, rwkv_a0, rwkv_a2, rwkv_g2, rwkv_k_k, rwkv_k_a, rwkv_r_k, lnx_w, lnx_b,
           w_o_attn, w_o_rwkv, w_out, w_mlp1, w_mlp2):
    Bc, Lc, D = x_prompt.shape
    Bs, Ls, _ = x_sample.shape
    depth = w_mod.shape[0]
    HA, dh = cache_attn_k.shape[3], cache_attn_k.shape[4]
    HR, n = state_rwkv_fwd.shape[2], state_rwkv_fwd.shape[3]
    DA, DR = HA * dh, HR * n
    lw, la, lg = rwkv_w2.shape[2], rwkv_a2.shape[2], rwkv_g2.shape[1]
    LW, LA, LG = _round_up(lw, LANES), _round_up(la, LANES), _round_up(lg, LANES)
    past = cache_attn_k.shape[2]
    rows = Ls // GRID_W

    xc = x_prompt.reshape(Bc * Lc, D)
    xs = x_sample.reshape(Bs * Ls, D)
    cond_rows = _round_up(Bs + 1, SUBLANES)
    cond = jnp.zeros((cond_rows, D), F32).at[:Bs].set(c).at[Bs].set(c_ctx)
    pad_c = lambda a, wdt: jnp.pad(a, [(0, 0)] * (a.ndim - 1) + [(0, wdt - a.shape[-1])])

    new_k, new_v, new_sf, new_sb = [], [], [], []
    for l in range(depth):
        mod3 = _adaln(cond, w_mod[l], b_mod[l]).reshape(cond_rows * 6, 1, D)

        wi = w_in[l]
        o1, o2 = 3 * DA, 3 * DA + 3 * DR
        w_a = wi[:, :o1].astype(BF16)
        hm = functools.partial(_head_minor, H=HR, n=n)
        rkv = wi[:, o1:o2].reshape(D, 3, DR)
        w_r = jnp.concatenate([hm(rkv).reshape(D, 3 * DR), pad_c(wi[:, o2:o2 + lw], LW),
                               pad_c(wi[:, o2 + lw:o2 + lw + la], LA),
                               pad_c(wi[:, o2 + lw + la:o2 + lw + la + lg], LG)], axis=1).astype(BF16)
        w_g = wi[:, o2 + lw + la + lg:].astype(BF16)
        mu = tshift_mu[l]
        mu_r = jnp.concatenate([hm(mu[:3 * DR].reshape(3, DR)).reshape(-1), pad_c(mu[3 * DR:3 * DR + lw], LW),
                                pad_c(mu[3 * DR + lw:3 * DR + lw + la], LA),
                                pad_c(mu[3 * DR + lw + la:], LG)]).reshape(1, -1)
        rw = dict(mu=mu_r, LW=LW, LA=LA,
                  w0=hm(rwkv_w0[l]).reshape(2, 1, DR), w2=_pad_rows(hm(rwkv_w2[l]), LW).astype(BF16),
                  a0=hm(rwkv_a0[l]).reshape(2, 1, DR), a2=_pad_rows(hm(rwkv_a2[l]), LA).astype(BF16),
                  g2=_pad_rows(hm(rwkv_g2[l]), LG).astype(BF16),
                  k_k=hm(rwkv_k_k[l]), k_a=hm(rwkv_k_a[l]), r_k=hm(rwkv_r_k[l].reshape(-1)),
                  lnx_w=hm(lnx_w[l]), lnx_b=hm(lnx_b[l]))
        w_oa = w_o_attn[l].astype(BF16)
        w_or = hm(w_o_rwkv[l].T).T.astype(BF16)
        w_o, w_1, w_2 = w_out[l].astype(BF16), w_mlp1[l].astype(BF16), w_mlp2[l].astype(BF16)
        bias_tab = _nbr_bias_table(attn_rpb[l], rows)

        def block(x, B, L, rows_per_cond, cond_base, attention, s_f0, s_b0, pa_dtype):
            h = _norm_modulate(x, norm1_w[l], mod3, (0, 1), rows_per_cond, cond_base)
            pa = _matmul(h, w_a, pa_dtype, name="proj_attn")
            pg = _matmul(h, w_g, F32, name="proj_gate")
            o_attn = attention(pa)
            o_rwkv, s_f, s_b = _rwkv_branch(h, w_r, B, L, HR, n, s_f0, s_b0, rw)
            merged = _merge(o_attn, w_oa, o_rwkv, w_or, pg)
            x = _matmul_gated_residual(merged, w_o, x, mod3, 2, rows_per_cond, cond_base, "out_proj")
            h = _norm_modulate(x, norm2_w[l], mod3, (3, 4), rows_per_cond, cond_base)
            a = _matmul(h, w_1, BF16, relu2=True, name="mlp_up")
            x = _matmul_gated_residual(a, w_2, x, mod3, 5, rows_per_cond, cond_base, "mlp_down")
            return x, pa, s_f, s_b

        z = jnp.zeros((Bc, HR, n, n), F32)
        xc, pa_c, s_f, s_b = block(
            xc, Bc, Lc, Bc * Lc, Bs,
            lambda pa: _context_attention(pa, Bc, Lc, HA, dh), z, z, F32)
        new_k.append(pa_c[:, DA:2 * DA].reshape(Bc, Lc, HA, dh))
        new_v.append(pa_c[:, 2 * DA:].reshape(Bc, Lc, HA, dh))
        new_sf.append(s_f)
        new_sb.append(s_b)

        kc = cache_attn_k[:, l].reshape(Bs * past, DA)
        vc = cache_attn_v[:, l].reshape(Bs * past, DA)
        xs, _, _, _ = block(
            xs, Bs, Ls, Ls, 0,
            lambda pa: _neighbourhood_attention(pa, kc, vc, bias_tab, Bs, Ls, HA, dh),
            state_rwkv_fwd[:, l], state_rwkv_bwd[:, l], BF16)

    y_prompt = _final_norm(xc, norm_f_w).reshape(Bc, Lc, D)
    y_sample = _final_norm(xs, norm_f_w).reshape(Bs, Ls, D)
    return (y_prompt, y_sample, jnp.stack(new_k, axis=1), jnp.stack(new_v, axis=1),
            jnp.stack(new_sf, axis=1), jnp.stack(new_sb, axis=1))
```

```python
import functools

import numpy as np
import jax
import jax.numpy as jnp
from jax import lax
from jax.experimental import pallas as pl
from jax.experimental.pallas import tpu as pltpu

GRID_W = 64
WIN_ROWS = 8
WIN_COLS = 16
RMS_EPS = 1e-6
LNX_EPS = 64e-5

LANES = 128
SUBLANES = 8
VMEM_LIMIT_BYTES = 56 * 1024 * 1024

F32 = jnp.float32
BF16 = jnp.bfloat16
NEG_BIG = -1e30


def _cparams(n_axes):
    return pltpu.CompilerParams(dimension_semantics=("arbitrary",) * n_axes,
                                vmem_limit_bytes=VMEM_LIMIT_BYTES)


def _tile(n, pref, unit=LANES):
    if n <= pref:
        return n
    t = (pref // unit) * unit
    while t > unit and n % t:
        t -= unit
    assert n % t == 0, (n, pref, unit)
    return t


def _round_up(n, m):
    return (n + m - 1) // m * m


def _sigmoid(x):
    return 1.0 / (1.0 + jnp.exp(-x))


def _softplus(x):
    return jnp.maximum(x, 0.0) + jnp.log(1.0 + jnp.exp(-jnp.abs(x)))


def _mod_kernel(c_ref, w_ref, b_ref, o_ref):
    c = c_ref[...]
    s = (c * _sigmoid(c)).astype(BF16)
    o_ref[...] = jnp.dot(s, w_ref[...].astype(BF16), preferred_element_type=F32) + b_ref[...]


def _adaln(cond, w_mod, b_mod):
    R, D = cond.shape
    N = w_mod.shape[1]
    tn = _tile(N, 512)
    return pl.pallas_call(
        _mod_kernel,
        grid=(N // tn,),
        in_specs=[pl.BlockSpec((R, D), lambda j: (0, 0)),
                  pl.BlockSpec((D, tn), lambda j: (0, j)),
                  pl.BlockSpec((1, tn), lambda j: (0, j))],
        out_specs=pl.BlockSpec((R, tn), lambda j: (0, j)),
        out_shape=jax.ShapeDtypeStruct((R, N), F32),
        compiler_params=_cparams(1),
        name="adaln",
    )(cond, w_mod, b_mod.reshape(1, N))


def _norm_mod_kernel(x_ref, w_ref, sh_ref, sc_ref, o_ref):
    x = x_ref[...]
    y = x * lax.rsqrt(jnp.mean(x * x, axis=-1, keepdims=True) + RMS_EPS) * w_ref[...]
    o_ref[...] = (y * (1.0 + sc_ref[...]) + sh_ref[...]).astype(o_ref.dtype)


def _norm_kernel(x_ref, w_ref, o_ref):
    x = x_ref[...]
    y = x * lax.rsqrt(jnp.mean(x * x, axis=-1, keepdims=True) + RMS_EPS) * w_ref[...]
    o_ref[...] = y.astype(o_ref.dtype)


def _mod_spec(part, tm, rows_per_cond, cond_base, D):
    return pl.BlockSpec((None, 1, D),
                        lambda i, *_: ((cond_base + (i * tm) // rows_per_cond) * 6 + part, 0, 0))


def _norm_modulate(x, w, mod3, parts, rows_per_cond, cond_base):
    T, D = x.shape
    tm = _tile(min(T, rows_per_cond), 256, SUBLANES)
    return pl.pallas_call(
        _norm_mod_kernel,
        grid=(T // tm,),
        in_specs=[pl.BlockSpec((tm, D), lambda i: (i, 0)),
                  pl.BlockSpec((1, D), lambda i: (0, 0)),
                  _mod_spec(parts[0], tm, rows_per_cond, cond_base, D),
                  _mod_spec(parts[1], tm, rows_per_cond, cond_base, D)],
        out_specs=pl.BlockSpec((tm, D), lambda i: (i, 0)),
        out_shape=jax.ShapeDtypeStruct((T, D), BF16),
        compiler_params=_cparams(1),
        name="norm_modulate",
    )(x, w.reshape(1, D), mod3, mod3)


def _final_norm(x, w):
    T, D = x.shape
    tm = _tile(T, 256, SUBLANES)
    return pl.pallas_call(
        _norm_kernel,
        grid=(T // tm,),
        in_specs=[pl.BlockSpec((tm, D), lambda i: (i, 0)),
                  pl.BlockSpec((1, D), lambda i: (0, 0))],
        out_specs=pl.BlockSpec((tm, D), lambda i: (i, 0)),
        out_shape=jax.ShapeDtypeStruct((T, D), F32),
        compiler_params=_cparams(1),
        name="final_norm",
    )(x, w.reshape(1, D))


def _mm_kernel(x_ref, w_ref, o_ref):
    o_ref[...] = jnp.dot(x_ref[...], w_ref[...], preferred_element_type=F32).astype(o_ref.dtype)


def _mm_relu2_kernel(x_ref, w_ref, o_ref):
    a = jnp.maximum(jnp.dot(x_ref[...], w_ref[...], preferred_element_type=F32), 0.0)
    o_ref[...] = (a * a).astype(o_ref.dtype)


def _matmul(x, w, out_dtype, *, relu2=False, tm_pref=1024, tn_pref=1024, name="matmul"):
    M, K = x.shape
    N = w.shape[1]
    tm, tn = _tile(M, tm_pref, SUBLANES), _tile(N, tn_pref)
    return pl.pallas_call(
        _mm_relu2_kernel if relu2 else _mm_kernel,
        grid=(M // tm, N // tn),
        in_specs=[pl.BlockSpec((tm, K), lambda i, j: (i, 0)),
                  pl.BlockSpec((K, tn), lambda i, j: (0, j))],
        out_specs=pl.BlockSpec((tm, tn), lambda i, j: (i, j)),
        out_shape=jax.ShapeDtypeStruct((M, N), out_dtype),
        compiler_params=_cparams(2),
        name=name,
    )(x, w)


def _merge_kernel(oa_ref, wa_ref, or_ref, wr_ref, ga_ref, gr_ref, o_ref):
    ya = jnp.dot(oa_ref[...], wa_ref[...], preferred_element_type=F32)
    yr = jnp.dot(or_ref[...], wr_ref[...], preferred_element_type=F32)
    o_ref[...] = (_sigmoid(ga_ref[...]) * ya + _sigmoid(gr_ref[...]) * yr).astype(o_ref.dtype)


def _merge(o_attn, w_oa, o_rwkv, w_or, pg):
    T, DA = o_attn.shape
    DR = o_rwkv.shape[1]
    D = w_oa.shape[1]
    tm, tn = _tile(T, 512, SUBLANES), _tile(D, 1024)
    nj = D // tn
    return pl.pallas_call(
        _merge_kernel,
        grid=(T // tm, nj),
        in_specs=[pl.BlockSpec((tm, DA), lambda i, j: (i, 0)),
                  pl.BlockSpec((DA, tn), lambda i, j: (0, j)),
                  pl.BlockSpec((tm, DR), lambda i, j: (i, 0)),
                  pl.BlockSpec((DR, tn), lambda i, j: (0, j)),
                  pl.BlockSpec((tm, tn), lambda i, j: (i, j)),
                  pl.BlockSpec((tm, tn), lambda i, j: (i, j + nj))],
        out_specs=pl.BlockSpec((tm, tn), lambda i, j: (i, j)),
        out_shape=jax.ShapeDtypeStruct((T, D), BF16),
        compiler_params=_cparams(2),
        name="merge",
    )(o_attn, w_oa, o_rwkv, w_or, pg, pg)


def _mm_resid_kernel(x_ref, w_ref, res_ref, gate_ref, o_ref, acc_ref, *, nk):
    k = pl.program_id(2)
    part = jnp.dot(x_ref[...], w_ref[...], preferred_element_type=F32)

    @pl.when(k == 0)
    def _():
        acc_ref[...] = part

    @pl.when(k > 0)
    def _():
        acc_ref[...] += part

    @pl.when(k == nk - 1)
    def _():
        o_ref[...] = res_ref[...] + gate_ref[...] * acc_ref[...]


def _matmul_gated_residual(x, w, res, mod3, part, rows_per_cond, cond_base, name):
    M, K = x.shape
    N = w.shape[1]
    tm = _tile(min(M, rows_per_cond), 1024, SUBLANES)
    tn, tk = _tile(N, 1024), _tile(K, 2048)
    nk = K // tk
    gate_spec = pl.BlockSpec(
        (None, 1, tn), lambda i, j, k: ((cond_base + (i * tm) // rows_per_cond) * 6 + part, 0, j))
    return pl.pallas_call(
        functools.partial(_mm_resid_kernel, nk=nk),
        grid=(M // tm, N // tn, nk),
        in_specs=[pl.BlockSpec((tm, tk), lambda i, j, k: (i, k)),
                  pl.BlockSpec((tk, tn), lambda i, j, k: (k, j)),
                  pl.BlockSpec((tm, tn), lambda i, j, k: (i, j)),
                  gate_spec],
        out_specs=pl.BlockSpec((tm, tn), lambda i, j, k: (i, j)),
        out_shape=jax.ShapeDtypeStruct((M, N), F32),
        scratch_shapes=[pltpu.VMEM((tm, tn), F32)],
        compiler_params=_cparams(3),
        name=name,
    )(x, w, res, mod3)


def _ctx_attn_kernel(q_ref, k_ref, v_ref, o_ref, *, scale):
    q = q_ref[...].astype(BF16)
    k = k_ref[...].astype(BF16)
    s = lax.dot_general(q, k, (((1,), (1,)), ((), ())), preferred_element_type=F32) * scale
    m = jnp.max(s, axis=-1, keepdims=True)
    e = jnp.exp(s - m)
    p = (e / jnp.sum(e, axis=-1, keepdims=True)).astype(BF16)
    o_ref[...] = jnp.dot(p, v_ref[...].astype(BF16), preferred_element_type=F32).astype(o_ref.dtype)


def _context_attention(pa, B, L, H, dh):
    spec = lambda off: pl.BlockSpec((L, dh), lambda b, h: (b, off + h))
    return pl.pallas_call(
        functools.partial(_ctx_attn_kernel, scale=dh ** -0.5),
        grid=(B, H),
        in_specs=[spec(0), spec(H), spec(2 * H)],
        out_specs=pl.BlockSpec((L, dh), lambda b, h: (b, h)),
        out_shape=jax.ShapeDtypeStruct((B * L, H * dh), BF16),
        compiler_params=_cparams(2),
        name="context_attention",
    )(pa, pa, pa)


def _nbr_bias_table(rpb, rows):
    kh = min(WIN_ROWS, rows)
    qc = np.arange(GRID_W)[:, None]
    kc = np.arange(GRID_W)[None, :]
    cs = np.clip(qc - WIN_COLS // 2, 0, GRID_W - WIN_COLS)
    col_ok = (kc >= cs) & (kc < cs + WIN_COLS)
    dc_idx = np.clip(kc - qc + WIN_COLS - 1, 0, 2 * WIN_COLS - 2)
    n_pat = 2 * WIN_ROWS - kh
    dr_idx = np.arange(n_pat)[:, None] + np.arange(kh)[None, :]
    tab = rpb[:, dr_idx][:, :, :, dc_idx]
    tab = jnp.where(col_ok[None, None, None], tab.astype(F32), NEG_BIG)
    tab = tab.transpose(1, 0, 3, 2, 4)
    return tab.reshape(n_pat, rpb.shape[0], GRID_W, kh * GRID_W)


def _nbr_attn_kernel(q_ref, k_ref, v_ref, kc_ref, vc_ref, bias_ref, o_ref, *, rows, kh, scale):
    kc = kc_ref[...].astype(BF16)
    vc = vc_ref[...].astype(BF16)
    nband = kh * GRID_W

    nr = 4 if rows % 4 == 0 else 1
    dn_t = (((1,), (1,)), ((), ()))

    def row_group(g, carry):
        q0 = pl.multiple_of(g * (nr * GRID_W), nr * GRID_W)
        q = q_ref[pl.ds(q0, nr * GRID_W), :].astype(BF16)
        s_ctx = lax.dot_general(q, kc, dn_t, preferred_element_type=F32) * scale
        s_loc, vbs = [], []
        for j in range(nr):
            r = g * nr + j
            rs = jnp.clip(r - WIN_ROWS // 2, 0, rows - kh)
            k0 = pl.multiple_of(rs * GRID_W, GRID_W)
            kb = k_ref[pl.ds(k0, nband), :].astype(BF16)
            vbs.append(v_ref[pl.ds(k0, nband), :].astype(BF16))
            s = lax.dot_general(q[j * GRID_W:(j + 1) * GRID_W], kb, dn_t, preferred_element_type=F32)
            s_loc.append(s * scale + bias_ref[rs - r + WIN_ROWS - 1])
        m_ctx = jnp.max(s_ctx, axis=-1, keepdims=True)
        p_loc, p_ctx = [], []
        for j in range(nr):
            sl = slice(j * GRID_W, (j + 1) * GRID_W)
            m = jnp.maximum(jnp.max(s_loc[j], axis=-1, keepdims=True), m_ctx[sl])
            e_loc = jnp.exp(s_loc[j] - m)
            e_ctx = jnp.exp(s_ctx[sl] - m)
            den = jnp.sum(e_loc, axis=-1, keepdims=True) + jnp.sum(e_ctx, axis=-1, keepdims=True)
            p_loc.append((e_loc / den).astype(BF16))
            p_ctx.append((e_ctx / den).astype(BF16))
        o_ctx = jnp.dot(jnp.concatenate(p_ctx, axis=0), vc, preferred_element_type=F32)
        for j in range(nr):
            o = jnp.dot(p_loc[j], vbs[j], preferred_element_type=F32) + o_ctx[j * GRID_W:(j + 1) * GRID_W]
            o_ref[pl.ds(q0 + j * GRID_W, GRID_W), :] = o.astype(o_ref.dtype)
        return carry

    lax.fori_loop(0, rows // nr, row_group, 0)


def _neighbourhood_attention(pa, k_ctx, v_ctx, bias_tab, B, N, H, dh):
    rows = N // GRID_W
    kh = min(WIN_ROWS, rows)
    Lc = k_ctx.shape[0] // B
    n_pat = bias_tab.shape[0]
    spec = lambda off: pl.BlockSpec((N, dh), lambda b, h: (b, off + h))
    cspec = pl.BlockSpec((Lc, dh), lambda b, h: (b, h))
    return pl.pallas_call(
        functools.partial(_nbr_attn_kernel, rows=rows, kh=kh, scale=dh ** -0.5),
        grid=(B, H),
        in_specs=[spec(0), spec(H), spec(2 * H), cspec, cspec,
                  pl.BlockSpec((n_pat, None, GRID_W, kh * GRID_W), lambda b, h: (0, h, 0, 0))],
        out_specs=pl.BlockSpec((N, dh), lambda b, h: (b, h)),
        out_shape=jax.ShapeDtypeStruct((B * N, H * dh), BF16),
        compiler_params=_cparams(2),
        name="neighbourhood_attention",
    )(pa, pa, pa, k_ctx, v_ctx, bias_tab)


def _head_sum(x, H):
    nt = x.shape[1] // LANES
    s = x[:, 0:LANES]
    for j in range(1, nt):
        s = s + x[:, j * LANES:(j + 1) * LANES]
    span = LANES // 2
    while span >= H:
        s = s + pltpu.roll(s, span, 1)
        span //= 2
    return jnp.concatenate([s] * nt, axis=1)


def _rwkv_pre_kernel(p_ref, prev_ref, next_ref, mu_ref, w0_ref, w2_ref, a0_ref, a2_ref, g2_ref,
                     kkw_ref, kaw_ref, rkw_ref,
                     r_ref, v_ref, kk_ref, dec_ref, b_ref, kd_ref, g_ref, bonus_ref,
                     *, tiles_per_seq, DR, LW, LA, H):
    i = pl.program_id(0)
    p = p_ref[...]
    first = (i % tiles_per_seq) == 0
    last = (i % tiles_per_seq) == tiles_per_seq - 1
    halo_prev = jnp.where(first, 0.0, prev_ref[...])
    halo_next = jnp.where(last, 0.0, next_ref[...])
    prev = jnp.concatenate([halo_prev, p[:-SUBLANES]], axis=0)
    nxt = jnp.concatenate([p[SUBLANES:], halo_next], axis=0)
    xr = p + mu_ref[...] * (0.5 * (prev + nxt) - p)
    r = xr[:, 0:DR]
    k = xr[:, DR:2 * DR]
    v = xr[:, 2 * DR:3 * DR]
    o = 3 * DR
    tw = jnp.tanh(xr[:, o:o + LW]).astype(BF16)
    xa = xr[:, o + LW:o + LW + LA].astype(BF16)
    sg = _sigmoid(xr[:, o + LW + LA:]).astype(BF16)
    kk = k * kkw_ref[...]
    kk = kk / jnp.maximum(jnp.sqrt(_head_sum(kk * kk, H)), 1e-12)
    r_ref[...] = r
    v_ref[...] = v
    kk_ref[...] = kk
    g_ref[...] = jnp.dot(sg, g2_ref[...], preferred_element_type=F32)
    rk = r * rkw_ref[...]
    bonus = None
    for d in range(2):
        wl = w0_ref[d] + jnp.dot(tw, w2_ref[d], preferred_element_type=F32)
        wl = -_softplus(-wl) - 0.5
        dec_ref[d] = jnp.exp(-jnp.exp(wl))
        a = _sigmoid(a0_ref[d] + jnp.dot(xa, a2_ref[d], preferred_element_type=F32))
        kd = k * (1.0 + (a - 1.0) * kaw_ref[...])
        kd_ref[d] = kd
        b_ref[d] = kk * a
        s = _head_sum(rk * kd, H)
        bonus = s if bonus is None else bonus + s
    bonus_ref[...] = bonus * v


def _rwkv_pre(pr, L, rw, DR, H):
    T, W = pr.shape
    hb = _tile(L, 16, 1)
    tl = hb * SUBLANES
    tiles_per_seq = L // hb
    nhb = T // SUBLANES
    LW, LA = rw["LW"], rw["LA"]
    LG = rw["g2"].shape[0]
    full = lambda shape: pl.BlockSpec(shape, lambda i: (0,) * len(shape))
    out_main = pl.BlockSpec((tl, DR), lambda i: (i, 0))
    out_dir = pl.BlockSpec((2, tl, DR), lambda i: (0, i, 0))
    sd_main = jax.ShapeDtypeStruct((T, DR), F32)
    sd_dir = jax.ShapeDtypeStruct((2, T, DR), F32)
    row = lambda w: w.reshape(1, DR)
    return pl.pallas_call(
        functools.partial(_rwkv_pre_kernel, tiles_per_seq=tiles_per_seq, DR=DR, LW=LW, LA=LA, H=H),
        grid=(T // tl,),
        in_specs=[pl.BlockSpec((tl, W), lambda i: (i, 0)),
                  pl.BlockSpec((SUBLANES, W), lambda i: (jnp.maximum(i * hb - 1, 0), 0)),
                  pl.BlockSpec((SUBLANES, W), lambda i: (jnp.minimum((i + 1) * hb, nhb - 1), 0)),
                  full((1, W)), full((2, 1, DR)), full((2, LW, DR)), full((2, 1, DR)),
                  full((2, LA, DR)), full((LG, DR)), full((1, DR)), full((1, DR)), full((1, DR))],
        out_specs=[out_main, out_main, out_main, out_dir, out_dir, out_dir, out_main, out_main],
        out_shape=[sd_main, sd_main, sd_main, sd_dir, sd_dir, sd_dir, sd_main, sd_main],
        compiler_params=_cparams(1),
        name="rwkv_pre",
    )(pr, pr, pr, rw["mu"], rw["w0"], rw["w2"], rw["a0"], rw["a2"], rw["g2"],
      row(rw["k_k"]), row(rw["k_a"]), row(rw["r_k"]))


N_KEY_OPS = 5


def _scan_value_tiles(nt):
    return min(4, nt)


def _scan_kernel(kk_ref, dec_ref, b_ref, kd_ref, r_ref, v_ref, e_ref, s0_ref, y_ref, sT_ref,
                 S_scr, opa_scr, opb_scr, nat_scr, lhs_scr, vy_scr, *, tb, nt, n, H):
    VL = LANES // H
    NT = n // VL
    VB = _scan_value_tiles(NT)
    NVG = NT // VB
    IPL = 2 if NVG % 2 == 0 else 1
    PPI = (VL // 2) // (NVG // IPL)
    d = pl.program_id(0)
    tblk = pl.program_id(2)
    key_refs = (kk_ref, dec_ref, b_ref, kd_ref, r_ref)

    @pl.when(tblk == 0)
    def _():
        S_scr[...] = s0_ref[...]

    def block_rows(i):
        t = i + d * (tb - 1 - 2 * i)
        return pl.ds(pl.multiple_of(t * SUBLANES, SUBLANES), SUBLANES)

    def stage(i):
        rows = block_rows(i)
        for oi, ref in enumerate(key_refs):
            x = ref[rows, :]
            for j in range(NT):
                nat_scr[pl.ds((oi * NT + j) * SUBLANES, SUBLANES), :] = x[:, j * LANES:(j + 1) * LANES]
        x = nat_scr[...]
        x1 = x.astype(BF16)
        r1 = x - x1.astype(F32)
        x2 = r1.astype(BF16)
        x3 = (r1 - x2.astype(F32)).astype(BF16)
        lhs_scr[...] = jnp.concatenate([x1, x2, x3], axis=1)

    def replicate_pair(op_dst, p):
        rep = jnp.dot(lhs_scr[...], e_ref[p], preferred_element_type=F32)
        for oi in range(N_KEY_OPS):
            for j in range(NT):
                r0 = (oi * NT + j) * SUBLANES
                op_dst[j * VL + 2 * p, oi] = rep[r0:r0 + SUBLANES, 0:LANES]
                op_dst[j * VL + 2 * p + 1, oi] = rep[r0:r0 + SUBLANES, LANES:2 * LANES]

    def one_step(i, op_cur, op_nxt):
        rows = block_rows(i)
        v_t = v_ref[rows, :]
        for j in range(NT):
            vy_scr[0, j] = v_t[:, j * LANES:(j + 1) * LANES]
        stage(jnp.minimum(i + 1, tb - 1))

        def value_group(vg):
            vts = [vg * VB + u for u in range(VB)]
            vals = [vy_scr[0, vt] for vt in vts]
            acc = [None for _ in vts]
            for k in range(n):
                kk_k = op_cur[k, 0]
                for u, vt in enumerate(vts):
                    p = S_scr[vg, k, u] * kk_k
                    acc[u] = p if acc[u] is None else acc[u] + p
            sa = [-a for a in acc]
            acc = [None for _ in vts]
            for k in range(n):
                w_k, b_k, kd_k, r_kk = op_cur[k, 1], op_cur[k, 2], op_cur[k, 3], op_cur[k, 4]
                for u, vt in enumerate(vts):
                    s_new = S_scr[vg, k, u] * w_k + sa[u] * b_k + vals[u] * kd_k
                    S_scr[vg, k, u] = s_new
                    p = s_new * r_kk
                    acc[u] = p if acc[u] is None else acc[u] + p
            for u, vt in enumerate(vts):
                vy_scr[1, vt] = acc[u]

        def sweep(li, c):
            for pp in range(PPI):
                replicate_pair(op_nxt, li * PPI + pp)
            for s in range(IPL):
                value_group(li * IPL + s)
            return c

        lax.fori_loop(0, NVG // IPL, sweep, 0)
        y_ref[rows, :] = jnp.concatenate([vy_scr[1, j] for j in range(NT)], axis=1)

    stage(0)
    for p in range(VL // 2):
        replicate_pair(opa_scr, p)

    def step_pair(ip, carry):
        one_step(2 * ip, opa_scr, opb_scr)
        one_step(2 * ip + 1, opb_scr, opa_scr)
        return carry

    lax.fori_loop(0, tb // 2, step_pair, 0)

    @pl.when(tblk == nt - 1)
    def _():
        sT_ref[...] = S_scr[...]


def _rwkv_scan(kk, dec, b, kd, r, v, s0, G, L, H, n):
    DR = H * n
    NT = DR // LANES
    VL = LANES // H
    tb = _tile(L, 32, 2)
    nt = L // tb
    src = np.arange(3 * LANES)[None, :, None] % LANES
    dst = np.arange(LANES)[None, None, :]
    e = src == np.arange(VL)[:, None, None] * H + dst % H
    e = jnp.asarray(np.concatenate([e[0::2], e[1::2]], axis=2), BF16)
    nrow = N_KEY_OPS * NT * SUBLANES
    tmap = lambda d, g, t: g * nt + t + d * (nt - 1 - 2 * t)
    seq = pl.BlockSpec((tb * SUBLANES, DR), lambda d, g, t: (tmap(d, g, t), 0))
    seq_d = pl.BlockSpec((None, tb * SUBLANES, DR), lambda d, g, t: (d, tmap(d, g, t), 0))
    st = pl.BlockSpec((None, None) + s0.shape[2:], lambda d, g, t: (d, g, 0, 0, 0, 0, 0))
    op_buf = pltpu.VMEM((n, N_KEY_OPS, SUBLANES, LANES), F32)
    return pl.pallas_call(
        functools.partial(_scan_kernel, tb=tb, nt=nt, n=n, H=H),
        grid=(2, G, nt),
        in_specs=[seq, seq_d, seq_d, seq_d, seq, seq,
                  pl.BlockSpec((VL // 2, 3 * LANES, 2 * LANES), lambda d, g, t: (0, 0, 0)), st],
        out_specs=[seq_d, st],
        out_shape=[jax.ShapeDtypeStruct((2, G * L * SUBLANES, DR), F32),
                   jax.ShapeDtypeStruct(s0.shape, F32)],
        scratch_shapes=[pltpu.VMEM(s0.shape[2:], F32), op_buf, op_buf,
                        pltpu.VMEM((nrow, LANES), F32), pltpu.VMEM((nrow, 3 * LANES), BF16),
                        pltpu.VMEM((2, NT, SUBLANES, LANES), F32)],
        compiler_params=_cparams(3),
        name="rwkv_scan",
    )(kk, dec, b, kd, r, v, e, s0)


def _gate_kernel(yf_ref, yb_ref, bonus_ref, g_ref, lw_ref, lb_ref, o_ref, *, H, n):
    tt, _, DR = yf_ref.shape
    lnw = lw_ref[...]
    lnb = lb_ref[...]

    def groupnorm(y):
        yc = y - _head_sum(y, H) * (1.0 / n)
        var = _head_sum(yc * yc, H) * (1.0 / n)
        return yc * lax.rsqrt(var + LNX_EPS) * lnw + lnb

    flat = lambda ref: ref[...].reshape(tt * SUBLANES, DR)
    o = (groupnorm(flat(yf_ref)) + groupnorm(flat(yb_ref)) + flat(bonus_ref)) * flat(g_ref)
    o_ref[...] = jnp.swapaxes(o.reshape(tt, SUBLANES, DR), 0, 1).astype(o_ref.dtype)


def _rwkv_gate(y_dirs, bonus, g, lnw, lnb, G, L, H, n):
    DR = g.shape[1]
    tt = _tile(L, 32, 16)
    nt = L // tt
    t3 = lambda a: a.reshape(a.shape[:-2] + (G * L, SUBLANES, DR))
    dspec = lambda d: pl.BlockSpec((None, tt, SUBLANES, DR), lambda gi, i: (d, gi * nt + i, 0, 0))
    mspec = pl.BlockSpec((tt, SUBLANES, DR), lambda gi, i: (gi * nt + i, 0, 0))
    wspec = pl.BlockSpec((1, DR), lambda gi, i: (0, 0))
    out = pl.pallas_call(
        functools.partial(_gate_kernel, H=H, n=n),
        grid=(G, nt),
        in_specs=[dspec(0), dspec(1), mspec, mspec, wspec, wspec],
        out_specs=pl.BlockSpec((SUBLANES, tt, DR), lambda gi, i: (gi, i, 0)),
        out_shape=jax.ShapeDtypeStruct((G * SUBLANES, L, DR), BF16),
        compiler_params=_cparams(2),
        name="rwkv_gate",
    )(t3(y_dirs), t3(y_dirs), t3(bonus), t3(g), lnw.reshape(1, DR), lnb.reshape(1, DR))
    return out.reshape(G * SUBLANES * L, DR)


def _mm_tmajor_kernel(x_ref, w_ref, o_ref, *, tt):
    x = x_ref[...]
    y = jnp.dot(x.reshape(SUBLANES * tt, x.shape[-1]), w_ref[...], preferred_element_type=F32)
    o_ref[...] = jnp.swapaxes(y.reshape(SUBLANES, tt, y.shape[-1]), 0, 1)


def _matmul_time_major(x, w, B, L):
    K = x.shape[1]
    N = w.shape[1]
    G = B // SUBLANES
    tt = _tile(L, 128, 16)
    tn = _tile(N, 512)
    nt = L // tt
    return pl.pallas_call(
        functools.partial(_mm_tmajor_kernel, tt=tt),
        grid=(G, nt, N // tn),
        in_specs=[pl.BlockSpec((SUBLANES, tt, K), lambda g, i, j: (g, i, 0)),
                  pl.BlockSpec((K, tn), lambda g, i, j: (0, j))],
        out_specs=pl.BlockSpec((tt, SUBLANES, tn), lambda g, i, j: (g * nt + i, 0, j)),
        out_shape=jax.ShapeDtypeStruct((G * L, SUBLANES, N), F32),
        compiler_params=_cparams(3),
        name="proj_rwkv",
    )(x.reshape(B, L, K), w).reshape(B * L, N)


def _head_minor(a, H, n):
    return a.reshape(a.shape[:-1] + (H, n)).swapaxes(-1, -2).reshape(a.shape)


def _rwkv_branch(h, w_r, B, L, H, n, s_f0, s_b0, rw):
    assert B % SUBLANES == 0 and LANES % (2 * H) == 0 and n % (LANES // H) == 0, (B, H, n)
    DR = H * n
    G = B // SUBLANES
    VL = LANES // H
    NT = n // VL
    pr = _matmul_time_major(h, w_r, B, L)
    r, v, kk, dec, b, kd, g, bonus = _rwkv_pre(pr, L, rw, DR, H)
    s0 = jnp.stack([s_f0, s_b0]).astype(F32).reshape(2, G, SUBLANES, H, NT, VL, n)
    VB = _scan_value_tiles(NT)
    s0 = jnp.transpose(s0, (0, 1, 4, 6, 2, 5, 3)).reshape(2, G, NT // VB, VB, n, SUBLANES, LANES)
    s0 = jnp.swapaxes(s0, 3, 4)
    y_dirs, sT = _rwkv_scan(kk, dec, b, kd, r, v, s0, G, L, H, n)
    o_rwkv = _rwkv_gate(y_dirs, bonus, g, rw["lnx_w"], rw["lnx_b"], G, L, H, n)
    sT = jnp.swapaxes(sT, 3, 4).reshape(2, G, NT, n, SUBLANES, VL, H)
    sT = jnp.transpose(sT, (0, 1, 4, 6, 2, 5, 3)).reshape(2, B, H, n, n)
    return o_rwkv, sT[0], sT[1]


def _pad_rows(w, rows):
    return jnp.pad(w, [(0, 0)] * (w.ndim - 2) + [(0, rows - w.shape[-2]), (0, 0)])


def kernel(x_prompt, x_sample, cache_attn_k, cache_attn_v, state_rwkv_fwd, state_rwkv_bwd, c, c_ctx,
           w_mod, b_mod, norm1_w, norm2_w, norm_f_w, w_in, tshift_mu, attn_rpb,
           rwkv_w0, rwkv_w2, rwkv_a0, rwkv_a2, rwkv_g2, rwkv_k_k, rwkv_k_a, rwkv_r_k, lnx_w, lnx_b,
           w_o_attn, w_o_rwkv, w_out, w_mlp1, w_mlp2):
    Bc, Lc, D = x_prompt.shape
    Bs, Ls, _ = x_sample.shape
    depth = w_mod.shape[0]
    HA, dh = cache_attn_k.shape[3], cache_attn_k.shape[4]
    HR, n = state_rwkv_fwd.shape[2], state_rwkv_fwd.shape[3]
    DA, DR = HA * dh, HR * n
    lw, la, lg = rwkv_w2.shape[2], rwkv_a2.shape[2], rwkv_g2.shape[1]
    LW, LA, LG = _round_up(lw, LANES), _round_up(la, LANES), _round_up(lg, LANES)
    past = cache_attn_k.shape[2]
    rows = Ls // GRID_W

    xc = x_prompt.reshape(Bc * Lc, D)
    xs = x_sample.reshape(Bs * Ls, D)
    cond_rows = _round_up(Bs + 1, SUBLANES)
    cond = jnp.zeros((cond_rows, D), F32).at[:Bs].set(c).at[Bs].set(c_ctx)
    pad_c = lambda a, wdt: jnp.pad(a, [(0, 0)] * (a.ndim - 1) + [(0, wdt - a.shape[-1])])

    new_k, new_v, new_sf, new_sb = [], [], [], []
    for l in range(depth):
        mod3 = _adaln(cond, w_mod[l], b_mod[l]).reshape(cond_rows * 6, 1, D)

        wi = w_in[l]
        o1, o2 = 3 * DA, 3 * DA + 3 * DR
        w_a = wi[:, :o1].astype(BF16)
        hm = functools.partial(_head_minor, H=HR, n=n)
        rkv = wi[:, o1:o2].reshape(D, 3, DR)
        w_r = jnp.concatenate([hm(rkv).reshape(D, 3 * DR), pad_c(wi[:, o2:o2 + lw], LW),
                               pad_c(wi[:, o2 + lw:o2 + lw + la], LA),
                               pad_c(wi[:, o2 + lw + la:o2 + lw + la + lg], LG)], axis=1).astype(BF16)
        w_g = wi[:, o2 + lw + la + lg:].astype(BF16)
        mu = tshift_mu[l]
        mu_r = jnp.concatenate([hm(mu[:3 * DR].reshape(3, DR)).reshape(-1), pad_c(mu[3 * DR:3 * DR + lw], LW),
                                pad_c(mu[3 * DR + lw:3 * DR + lw + la], LA),
                                pad_c(mu[3 * DR + lw + la:], LG)]).reshape(1, -1)
        rw = dict(mu=mu_r, LW=LW, LA=LA,
                  w0=hm(rwkv_w0[l]).reshape(2, 1, DR), w2=_pad_rows(hm(rwkv_w2[l]), LW).astype(BF16),
                  a0=hm(rwkv_a0[l]).reshape(2, 1, DR), a2=_pad_rows(hm(rwkv_a2[l]), LA).astype(BF16),
                  g2=_pad_rows(hm(rwkv_g2[l]), LG).astype(BF16),
                  k_k=hm(rwkv_k_k[l]), k_a=hm(rwkv_k_a[l]), r_k=hm(rwkv_r_k[l].reshape(-1)),
                  lnx_w=hm(lnx_w[l]), lnx_b=hm(lnx_b[l]))
        w_oa = w_o_attn[l].astype(BF16)
        w_or = hm(w_o_rwkv[l].T).T.astype(BF16)
        w_o, w_1, w_2 = w_out[l].astype(BF16), w_mlp1[l].astype(BF16), w_mlp2[l].astype(BF16)
        bias_tab = _nbr_bias_table(attn_rpb[l], rows)

        def block(x, B, L, rows_per_cond, cond_base, attention, s_f0, s_b0, pa_dtype):
            h = _norm_modulate(x, norm1_w[l], mod3, (0, 1), rows_per_cond, cond_base)
            pa = _matmul(h, w_a, pa_dtype, name="proj_attn")
            pg = _matmul(h, w_g, F32, name="proj_gate")
            o_attn = attention(pa)
            o_rwkv, s_f, s_b = _rwkv_branch(h, w_r, B, L, HR, n, s_f0, s_b0, rw)
            merged = _merge(o_attn, w_oa, o_rwkv, w_or, pg)
            x = _matmul_gated_residual(merged, w_o, x, mod3, 2, rows_per_cond, cond_base, "out_proj")
            h = _norm_modulate(x, norm2_w[l], mod3, (3, 4), rows_per_cond, cond_base)
            a = _matmul(h, w_1, BF16, relu2=True, name="mlp_up")
            x = _matmul_gated_residual(a, w_2, x, mod3, 5, rows_per_cond, cond_base, "mlp_down")
            return x, pa, s_f, s_b

        z = jnp.zeros((Bc, HR, n, n), F32)
        xc, pa_c, s_f, s_b = block(
            xc, Bc, Lc, Bc * Lc, Bs,
            lambda pa: _context_attention(pa, Bc, Lc, HA, dh), z, z, F32)
        new_k.append(pa_c[:, DA:2 * DA].reshape(Bc, Lc, HA, dh))
        new_v.append(pa_c[:, 2 * DA:].reshape(Bc, Lc, HA, dh))
        new_sf.append(s_f)
        new_sb.append(s_b)

        kc = cache_attn_k[:, l].reshape(Bs * past, DA)
        vc = cache_attn_v[:, l].reshape(Bs * past, DA)
        xs, _, _, _ = block(
            xs, Bs, Ls, Ls, 0,
            lambda pa: _neighbourhood_attention(pa, kc, vc, bias_tab, Bs, Ls, HA, dh),
            state_rwkv_fwd[:, l], state_rwkv_bwd[:, l], BF16)

    y_prompt = _final_norm(xc, norm_f_w).reshape(Bc, Lc, D)
    y_sample = _final_norm(xs, norm_f_w).reshape(Bs, Ls, D)
    return (y_prompt, y_sample, jnp.stack(new_k, axis=1), jnp.stack(new_v, axis=1),
            jnp.stack(new_sf, axis=1), jnp.stack(new_sb, axis=1))
```

```python
import functools

import numpy as np
import jax
import jax.numpy as jnp
from jax import lax
from jax.experimental import pallas as pl
from jax.experimental.pallas import tpu as pltpu

GRID_W = 64
WIN_ROWS = 8
WIN_COLS = 16
RMS_EPS = 1e-6
LNX_EPS = 64e-5

LANES = 128
SUBLANES = 8
VMEM_LIMIT_BYTES = 56 * 1024 * 1024

F32 = jnp.float32
BF16 = jnp.bfloat16
NEG_BIG = -1e30


def _cparams(n_axes):
    return pltpu.CompilerParams(dimension_semantics=("arbitrary",) * n_axes,
                                vmem_limit_bytes=VMEM_LIMIT_BYTES)


def _tile(n, pref, unit=LANES):
    if n <= pref:
        return n
    t = (pref // unit) * unit
    while t > unit and n % t:
        t -= unit
    assert n % t == 0, (n, pref, unit)
    return t


def _round_up(n, m):
    return (n + m - 1) // m * m


def _sigmoid(x):
    return 1.0 / (1.0 + jnp.exp(-x))


def _softplus(x):
    return jnp.maximum(x, 0.0) + jnp.log(1.0 + jnp.exp(-jnp.abs(x)))


def _mod_kernel(c_ref, w_ref, b_ref, o_ref):
    c = c_ref[...]
    s = (c * _sigmoid(c)).astype(BF16)
    o_ref[...] = jnp.dot(s, w_ref[...].astype(BF16), preferred_element_type=F32) + b_ref[...]


def _adaln(cond, w_mod, b_mod):
    R, D = cond.shape
    N = w_mod.shape[1]
    tn = _tile(N, 512)
    return pl.pallas_call(
        _mod_kernel,
        grid=(N // tn,),
        in_specs=[pl.BlockSpec((R, D), lambda j: (0, 0)),
                  pl.BlockSpec((D, tn), lambda j: (0, j)),
                  pl.BlockSpec((1, tn), lambda j: (0, j))],
        out_specs=pl.BlockSpec((R, tn), lambda j: (0, j)),
        out_shape=jax.ShapeDtypeStruct((R, N), F32),
        compiler_params=_cparams(1),
        name="adaln",
    )(cond, w_mod, b_mod.reshape(1, N))


def _norm_mod_kernel(x_ref, w_ref, sh_ref, sc_ref, o_ref):
    x = x_ref[...]
    y = x * lax.rsqrt(jnp.mean(x * x, axis=-1, keepdims=True) + RMS_EPS) * w_ref[...]
    o_ref[...] = (y * (1.0 + sc_ref[...]) + sh_ref[...]).astype(o_ref.dtype)


def _norm_kernel(x_ref, w_ref, o_ref):
    x = x_ref[...]
    y = x * lax.rsqrt(jnp.mean(x * x, axis=-1, keepdims=True) + RMS_EPS) * w_ref[...]
    o_ref[...] = y.astype(o_ref.dtype)


def _mod_spec(part, tm, rows_per_cond, cond_base, D):
    return pl.BlockSpec((None, 1, D),
                        lambda i, *_: ((cond_base + (i * tm) // rows_per_cond) * 6 + part, 0, 0))


def _norm_modulate(x, w, mod3, parts, rows_per_cond, cond_base):
    T, D = x.shape
    tm = _tile(min(T, rows_per_cond), 256, SUBLANES)
    return pl.pallas_call(
        _norm_mod_kernel,
        grid=(T // tm,),
        in_specs=[pl.BlockSpec((tm, D), lambda i: (i, 0)),
                  pl.BlockSpec((1, D), lambda i: (0, 0)),
                  _mod_spec(parts[0], tm, rows_per_cond, cond_base, D),
                  _mod_spec(parts[1], tm, rows_per_cond, cond_base, D)],
        out_specs=pl.BlockSpec((tm, D), lambda i: (i, 0)),
        out_shape=jax.ShapeDtypeStruct((T, D), BF16),
        compiler_params=_cparams(1),
        name="norm_modulate",
    )(x, w.reshape(1, D), mod3, mod3)


def _final_norm(x, w):
    T, D = x.shape
    tm = _tile(T, 256, SUBLANES)
    return pl.pallas_call(
        _norm_kernel,
        grid=(T // tm,),
        in_specs=[pl.BlockSpec((tm, D), lambda i: (i, 0)),
                  pl.BlockSpec((1, D), lambda i: (0, 0))],
        out_specs=pl.BlockSpec((tm, D), lambda i: (i, 0)),
        out_shape=jax.ShapeDtypeStruct((T, D), F32),
        compiler_params=_cparams(1),
        name="final_norm",
    )(x, w.reshape(1, D))


def _mm_kernel(x_ref, w_ref, o_ref):
    o_ref[...] = jnp.dot(x_ref[...], w_ref[...], preferred_element_type=F32).astype(o_ref.dtype)


def _mm_relu2_kernel(x_ref, w_ref, o_ref):
    a = jnp.maximum(jnp.dot(x_ref[...], w_ref[...], preferred_element_type=F32), 0.0)
    o_ref[...] = (a * a).astype(o_ref.dtype)


def _matmul(x, w, out_dtype, *, relu2=False, tm_pref=1024, tn_pref=1024, name="matmul"):
    M, K = x.shape
    N = w.shape[1]
    tm, tn = _tile(M, tm_pref, SUBLANES), _tile(N, tn_pref)
    return pl.pallas_call(
        _mm_relu2_kernel if relu2 else _mm_kernel,
        grid=(M // tm, N // tn),
        in_specs=[pl.BlockSpec((tm, K), lambda i, j: (i, 0)),
                  pl.BlockSpec((K, tn), lambda i, j: (0, j))],
        out_specs=pl.BlockSpec((tm, tn), lambda i, j: (i, j)),
        out_shape=jax.ShapeDtypeStruct((M, N), out_dtype),
        compiler_params=_cparams(2),
        name=name,
    )(x, w)


def _merge_kernel(oa_ref, wa_ref, or_ref, wr_ref, ga_ref, gr_ref, o_ref):
    ya = jnp.dot(oa_ref[...], wa_ref[...], preferred_element_type=F32)
    yr = jnp.dot(or_ref[...], wr_ref[...], preferred_element_type=F32)
    o_ref[...] = (_sigmoid(ga_ref[...]) * ya + _sigmoid(gr_ref[...]) * yr).astype(o_ref.dtype)


def _merge(o_attn, w_oa, o_rwkv, w_or, pg):
    T, DA = o_attn.shape
    DR = o_rwkv.shape[1]
    D = w_oa.shape[1]
    tm, tn = _tile(T, 512, SUBLANES), _tile(D, 1024)
    nj = D // tn
    return pl.pallas_call(
        _merge_kernel,
        grid=(T // tm, nj),
        in_specs=[pl.BlockSpec((tm, DA), lambda i, j: (i, 0)),
                  pl.BlockSpec((DA, tn), lambda i, j: (0, j)),
                  pl.BlockSpec((tm, DR), lambda i, j: (i, 0)),
                  pl.BlockSpec((DR, tn), lambda i, j: (0, j)),
                  pl.BlockSpec((tm, tn), lambda i, j: (i, j)),
                  pl.BlockSpec((tm, tn), lambda i, j: (i, j + nj))],
        out_specs=pl.BlockSpec((tm, tn), lambda i, j: (i, j)),
        out_shape=jax.ShapeDtypeStruct((T, D), BF16),
        compiler_params=_cparams(2),
        name="merge",
    )(o_attn, w_oa, o_rwkv, w_or, pg, pg)


def _mm_resid_kernel(x_ref, w_ref, res_ref, gate_ref, o_ref, acc_ref, *, nk):
    k = pl.program_id(2)
    part = jnp.dot(x_ref[...], w_ref[...], preferred_element_type=F32)

    @pl.when(k == 0)
    def _():
        acc_ref[...] = part

    @pl.when(k > 0)
    def _():
        acc_ref[...] += part

    @pl.when(k == nk - 1)
    def _():
        o_ref[...] = res_ref[...] + gate_ref[...] * acc_ref[...]


def _matmul_gated_residual(x, w, res, mod3, part, rows_per_cond, cond_base, name):
    M, K = x.shape
    N = w.shape[1]
    tm = _tile(min(M, rows_per_cond), 1024, SUBLANES)
    tn, tk = _tile(N, 1024), _tile(K, 2048)
    nk = K // tk
    gate_spec = pl.BlockSpec(
        (None, 1, tn), lambda i, j, k: ((cond_base + (i * tm) // rows_per_cond) * 6 + part, 0, j))
    return pl.pallas_call(
        functools.partial(_mm_resid_kernel, nk=nk),
        grid=(M // tm, N // tn, nk),
        in_specs=[pl.BlockSpec((tm, tk), lambda i, j, k: (i, k)),
                  pl.BlockSpec((tk, tn), lambda i, j, k: (k, j)),
                  pl.BlockSpec((tm, tn), lambda i, j, k: (i, j)),
                  gate_spec],
        out_specs=pl.BlockSpec((tm, tn), lambda i, j, k: (i, j)),
        out_shape=jax.ShapeDtypeStruct((M, N), F32),
        scratch_shapes=[pltpu.VMEM((tm, tn), F32)],
        compiler_params=_cparams(3),
        name=name,
    )(x, w, res, mod3)


def _ctx_attn_kernel(q_ref, k_ref, v_ref, o_ref, *, scale, H, dh):
    for h in range(H):
        cols = slice(h * dh, (h + 1) * dh)
        q = q_ref[:, cols].astype(BF16)
        k = k_ref[:, cols].astype(BF16)
        s = lax.dot_general(q, k, (((1,), (1,)), ((), ())), preferred_element_type=F32) * scale
        m = jnp.max(s, axis=-1, keepdims=True)
        e = jnp.exp(s - m)
        p = (e / jnp.sum(e, axis=-1, keepdims=True)).astype(BF16)
        o_ref[:, cols] = jnp.dot(p, v_ref[:, cols].astype(BF16),
                                 preferred_element_type=F32).astype(o_ref.dtype)


def _context_attention(pa, B, L, H, dh):
    spec = lambda part: pl.BlockSpec((L, H * dh), lambda b: (b, part))
    return pl.pallas_call(
        functools.partial(_ctx_attn_kernel, scale=dh ** -0.5, H=H, dh=dh),
        grid=(B,),
        in_specs=[spec(0), spec(1), spec(2)],
        out_specs=pl.BlockSpec((L, H * dh), lambda b: (b, 0)),
        out_shape=jax.ShapeDtypeStruct((B * L, H * dh), BF16),
        compiler_params=_cparams(1),
        name="context_attention",
    )(pa, pa, pa)


def _nbr_bias_table(rpb, rows):
    kh = min(WIN_ROWS, rows)
    qc = np.arange(GRID_W)[:, None]
    kc = np.arange(GRID_W)[None, :]
    cs = np.clip(qc - WIN_COLS // 2, 0, GRID_W - WIN_COLS)
    col_ok = (kc >= cs) & (kc < cs + WIN_COLS)
    dc_idx = np.clip(kc - qc + WIN_COLS - 1, 0, 2 * WIN_COLS - 2)
    n_pat = 2 * WIN_ROWS - kh
    dr_idx = np.arange(n_pat)[:, None] + np.arange(kh)[None, :]
    tab = rpb[:, dr_idx][:, :, :, dc_idx]
    tab = jnp.where(col_ok[None, None, None], tab.astype(F32), NEG_BIG)
    tab = tab.transpose(1, 0, 3, 2, 4)
    return tab.reshape(n_pat, rpb.shape[0], GRID_W, kh * GRID_W)


def _nbr_attn_kernel(q_ref, k_ref, v_ref, kc_ref, vc_ref, bias_ref, o_ref, *, rows, kh, scale):
    kc = kc_ref[...].astype(BF16)
    vc = vc_ref[...].astype(BF16)
    nband = kh * GRID_W

    nr = 4 if rows % 4 == 0 else 1
    dn_t = (((1,), (1,)), ((), ()))

    def row_group(g, carry):
        q0 = pl.multiple_of(g * (nr * GRID_W), nr * GRID_W)
        q = q_ref[pl.ds(q0, nr * GRID_W), :].astype(BF16)
        s_ctx = lax.dot_general(q, kc, dn_t, preferred_element_type=F32) * scale
        s_loc, vbs = [], []
        for j in range(nr):
            r = g * nr + j
            rs = jnp.clip(r - WIN_ROWS // 2, 0, rows - kh)
            k0 = pl.multiple_of(rs * GRID_W, GRID_W)
            kb = k_ref[pl.ds(k0, nband), :].astype(BF16)
            vbs.append(v_ref[pl.ds(k0, nband), :].astype(BF16))
            s = lax.dot_general(q[j * GRID_W:(j + 1) * GRID_W], kb, dn_t, preferred_element_type=F32)
            s_loc.append(s * scale + bias_ref[rs - r + WIN_ROWS - 1])
        m_ctx = jnp.max(s_ctx, axis=-1, keepdims=True)
        p_loc, p_ctx = [], []
        for j in range(nr):
            sl = slice(j * GRID_W, (j + 1) * GRID_W)
            m = jnp.maximum(jnp.max(s_loc[j], axis=-1, keepdims=True), m_ctx[sl])
            e_loc = jnp.exp(s_loc[j] - m)
            e_ctx = jnp.exp(s_ctx[sl] - m)
            den = jnp.sum(e_loc, axis=-1, keepdims=True) + jnp.sum(e_ctx, axis=-1, keepdims=True)
            p_loc.append((e_loc / den).astype(BF16))
            p_ctx.append((e_ctx / den).astype(BF16))
        o_ctx = jnp.dot(jnp.concatenate(p_ctx, axis=0), vc, preferred_element_type=F32)
        for j in range(nr):
            o = jnp.dot(p_loc[j], vbs[j], preferred_element_type=F32) + o_ctx[j * GRID_W:(j + 1) * GRID_W]
            o_ref[pl.ds(q0 + j * GRID_W, GRID_W), :] = o.astype(o_ref.dtype)
        return carry

    lax.fori_loop(0, rows // nr, row_group, 0)


def _neighbourhood_attention(pa, k_ctx, v_ctx, bias_tab, B, N, H, dh):
    rows = N // GRID_W
    kh = min(WIN_ROWS, rows)
    Lc = k_ctx.shape[0] // B
    n_pat = bias_tab.shape[0]
    spec = lambda off: pl.BlockSpec((N, dh), lambda b, h: (b, off + h))
    cspec = pl.BlockSpec((Lc, dh), lambda b, h: (b, h))
    return pl.pallas_call(
        functools.partial(_nbr_attn_kernel, rows=rows, kh=kh, scale=dh ** -0.5),
        grid=(B, H),
        in_specs=[spec(0), spec(H), spec(2 * H), cspec, cspec,
                  pl.BlockSpec((n_pat, None, GRID_W, kh * GRID_W), lambda b, h: (0, h, 0, 0))],
        out_specs=pl.BlockSpec((N, dh), lambda b, h: (b, h)),
        out_shape=jax.ShapeDtypeStruct((B * N, H * dh), BF16),
        compiler_params=_cparams(2),
        name="neighbourhood_attention",
    )(pa, pa, pa, k_ctx, v_ctx, bias_tab)


def _head_sum(x, H):
    nt = x.shape[1] // LANES
    s = x[:, 0:LANES]
    for j in range(1, nt):
        s = s + x[:, j * LANES:(j + 1) * LANES]
    span = LANES // 2
    while span >= H:
        s = s + pltpu.roll(s, span, 1)
        span //= 2
    return jnp.concatenate([s] * nt, axis=1)


def _rwkv_pre_kernel(p_ref, prev_ref, next_ref, mu_ref, w0_ref, w2_ref, a0_ref, a2_ref, g2_ref,
                     kkw_ref, kaw_ref, rkw_ref,
                     r_ref, v_ref, kk_ref, dec_ref, b_ref, kd_ref, g_ref, bonus_ref,
                     *, tiles_per_seq, DR, LW, LA, H):
    i = pl.program_id(0)
    p = p_ref[...]
    first = (i % tiles_per_seq) == 0
    last = (i % tiles_per_seq) == tiles_per_seq - 1
    halo_prev = jnp.where(first, 0.0, prev_ref[...])
    halo_next = jnp.where(last, 0.0, next_ref[...])
    prev = jnp.concatenate([halo_prev, p[:-SUBLANES]], axis=0)
    nxt = jnp.concatenate([p[SUBLANES:], halo_next], axis=0)
    xr = p + mu_ref[...] * (0.5 * (prev + nxt) - p)
    r = xr[:, 0:DR]
    k = xr[:, DR:2 * DR]
    v = xr[:, 2 * DR:3 * DR]
    o = 3 * DR
    tw = jnp.tanh(xr[:, o:o + LW]).astype(BF16)
    xa = xr[:, o + LW:o + LW + LA].astype(BF16)
    sg = _sigmoid(xr[:, o + LW + LA:]).astype(BF16)
    kk = k * kkw_ref[...]
    kk = kk / jnp.maximum(jnp.sqrt(_head_sum(kk * kk, H)), 1e-12)
    r_ref[...] = r
    v_ref[...] = v
    kk_ref[...] = kk
    g_ref[...] = jnp.dot(sg, g2_ref[...], preferred_element_type=F32)
    rk = r * rkw_ref[...]
    bonus = None
    for d in range(2):
        wl = w0_ref[d] + jnp.dot(tw, w2_ref[d], preferred_element_type=F32)
        wl = -_softplus(-wl) - 0.5
        dec_ref[d] = jnp.exp(-jnp.exp(wl))
        a = _sigmoid(a0_ref[d] + jnp.dot(xa, a2_ref[d], preferred_element_type=F32))
        kd = k * (1.0 + (a - 1.0) * kaw_ref[...])
        kd_ref[d] = kd
        b_ref[d] = kk * a
        s = _head_sum(rk * kd, H)
        bonus = s if bonus is None else bonus + s
    bonus_ref[...] = bonus * v


def _rwkv_pre(pr, L, rw, DR, H):
    T, W = pr.shape
    hb = _tile(L, 16, 1)
    tl = hb * SUBLANES
    tiles_per_seq = L // hb
    nhb = T // SUBLANES
    LW, LA = rw["LW"], rw["LA"]
    LG = rw["g2"].shape[0]
    full = lambda shape: pl.BlockSpec(shape, lambda i: (0,) * len(shape))
    out_main = pl.BlockSpec((tl, DR), lambda i: (i, 0))
    out_dir = pl.BlockSpec((2, tl, DR), lambda i: (0, i, 0))
    sd_main = jax.ShapeDtypeStruct((T, DR), F32)
    sd_dir = jax.ShapeDtypeStruct((2, T, DR), F32)
    row = lambda w: w.reshape(1, DR)
    return pl.pallas_call(
        functools.partial(_rwkv_pre_kernel, tiles_per_seq=tiles_per_seq, DR=DR, LW=LW, LA=LA, H=H),
        grid=(T // tl,),
        in_specs=[pl.BlockSpec((tl, W), lambda i: (i, 0)),
                  pl.BlockSpec((SUBLANES, W), lambda i: (jnp.maximum(i * hb - 1, 0), 0)),
                  pl.BlockSpec((SUBLANES, W), lambda i: (jnp.minimum((i + 1) * hb, nhb - 1), 0)),
                  full((1, W)), full((2, 1, DR)), full((2, LW, DR)), full((2, 1, DR)),
                  full((2, LA, DR)), full((LG, DR)), full((1, DR)), full((1, DR)), full((1, DR))],
        out_specs=[out_main, out_main, out_main, out_dir, out_dir, out_dir, out_main, out_main],
        out_shape=[sd_main, sd_main, sd_main, sd_dir, sd_dir, sd_dir, sd_main, sd_main],
        compiler_params=_cparams(1),
        name="rwkv_pre",
    )(pr, pr, pr, rw["mu"], rw["w0"], rw["w2"], rw["a0"], rw["a2"], rw["g2"],
      row(rw["k_k"]), row(rw["k_a"]), row(rw["r_k"]))


N_KEY_OPS = 5


def _scan_value_tiles(nt):
    return min(4, nt)


def _scan_kernel(kk_ref, dec_ref, b_ref, kd_ref, r_ref, v_ref, e_ref, s0_ref, y_ref, sT_ref,
                 S_scr, opa_scr, opb_scr, nat_scr, lhs_scr, vy_scr, *, tb, nt, n, H):
    VL = LANES // H
    NT = n // VL
    VB = _scan_value_tiles(NT)
    NVG = NT // VB
    IPL = 2 if NVG % 2 == 0 else 1
    PPI = (VL // 2) // (NVG // IPL)
    d = pl.program_id(0)
    tblk = pl.program_id(2)
    key_refs = (kk_ref, dec_ref, b_ref, kd_ref, r_ref)

    @pl.when(tblk == 0)
    def _():
        S_scr[...] = s0_ref[...]

    def block_rows(i):
        t = i + d * (tb - 1 - 2 * i)
        return pl.ds(pl.multiple_of(t * SUBLANES, SUBLANES), SUBLANES)

    def stage(i):
        rows = block_rows(i)
        for oi, ref in enumerate(key_refs):
            x = ref[rows, :]
            for j in range(NT):
                nat_scr[pl.ds((oi * NT + j) * SUBLANES, SUBLANES), :] = x[:, j * LANES:(j + 1) * LANES]
        x = nat_scr[...]
        x1 = x.astype(BF16)
        r1 = x - x1.astype(F32)
        x2 = r1.astype(BF16)
        x3 = (r1 - x2.astype(F32)).astype(BF16)
        lhs_scr[...] = jnp.concatenate([x1, x2, x3], axis=1)

    def replicate_pair(op_dst, p):
        rep = jnp.dot(lhs_scr[...], e_ref[p], preferred_element_type=F32)
        for oi in range(N_KEY_OPS):
            for j in range(NT):
                r0 = (oi * NT + j) * SUBLANES
                op_dst[j * VL + 2 * p, oi] = rep[r0:r0 + SUBLANES, 0:LANES]
                op_dst[j * VL + 2 * p + 1, oi] = rep[r0:r0 + SUBLANES, LANES:2 * LANES]

    def one_step(i, op_cur, op_nxt):
        rows = block_rows(i)
        v_t = v_ref[rows, :]
        for j in range(NT):
            vy_scr[0, j] = v_t[:, j * LANES:(j + 1) * LANES]
        stage(jnp.minimum(i + 1, tb - 1))

        def value_group(vg):
            vts = [vg * VB + u for u in range(VB)]
            vals = [vy_scr[0, vt] for vt in vts]
            acc = [None for _ in vts]
            for k in range(n):
                kk_k = op_cur[k, 0]
                for u, vt in enumerate(vts):
                    p = S_scr[vg, k, u] * kk_k
                    acc[u] = p if acc[u] is None else acc[u] + p
            sa = [-a for a in acc]
            acc = [None for _ in vts]
            for k in range(n):
                w_k, b_k, kd_k, r_kk = op_cur[k, 1], op_cur[k, 2], op_cur[k, 3], op_cur[k, 4]
                for u, vt in enumerate(vts):
                    s_new = S_scr[vg, k, u] * w_k + sa[u] * b_k + vals[u] * kd_k
                    S_scr[vg, k, u] = s_new
                    p = s_new * r_kk
                    acc[u] = p if acc[u] is None else acc[u] + p
            for u, vt in enumerate(vts):
                vy_scr[1, vt] = acc[u]

        def sweep(li, c):
            for pp in range(PPI):
                replicate_pair(op_nxt, li * PPI + pp)
            for s in range(IPL):
                value_group(li * IPL + s)
            return c

        lax.fori_loop(0, NVG // IPL, sweep, 0)
        y_ref[rows, :] = jnp.concatenate([vy_scr[1, j] for j in range(NT)], axis=1)

    stage(0)
    for p in range(VL // 2):
        replicate_pair(opa_scr, p)

    def step_pair(ip, carry):
        one_step(2 * ip, opa_scr, opb_scr)
        one_step(2 * ip + 1, opb_scr, opa_scr)
        return carry

    lax.fori_loop(0, tb // 2, step_pair, 0)

    @pl.when(tblk == nt - 1)
    def _():
        sT_ref[...] = S_scr[...]


def _rwkv_scan(kk, dec, b, kd, r, v, s0, G, L, H, n):
    DR = H * n
    NT = DR // LANES
    VL = LANES // H
    tb = _tile(L, 32, 2)
    nt = L // tb
    src = np.arange(3 * LANES)[None, :, None] % LANES
    dst = np.arange(LANES)[None, None, :]
    e = src == np.arange(VL)[:, None, None] * H + dst % H
    e = jnp.asarray(np.concatenate([e[0::2], e[1::2]], axis=2), BF16)
    nrow = N_KEY_OPS * NT * SUBLANES
    tmap = lambda d, g, t: g * nt + t + d * (nt - 1 - 2 * t)
    seq = pl.BlockSpec((tb * SUBLANES, DR), lambda d, g, t: (tmap(d, g, t), 0))
    seq_d = pl.BlockSpec((None, tb * SUBLANES, DR), lambda d, g, t: (d, tmap(d, g, t), 0))
    st = pl.BlockSpec((None, None) + s0.shape[2:], lambda d, g, t: (d, g, 0, 0, 0, 0, 0))
    op_buf = pltpu.VMEM((n, N_KEY_OPS, SUBLANES, LANES), F32)
    return pl.pallas_call(
        functools.partial(_scan_kernel, tb=tb, nt=nt, n=n, H=H),
        grid=(2, G, nt),
        in_specs=[seq, seq_d, seq_d, seq_d, seq, seq,
                  pl.BlockSpec((VL // 2, 3 * LANES, 2 * LANES), lambda d, g, t: (0, 0, 0)), st],
        out_specs=[seq_d, st],
        out_shape=[jax.ShapeDtypeStruct((2, G * L * SUBLANES, DR), F32),
                   jax.ShapeDtypeStruct(s0.shape, F32)],
        scratch_shapes=[pltpu.VMEM(s0.shape[2:], F32), op_buf, op_buf,
                        pltpu.VMEM((nrow, LANES), F32), pltpu.VMEM((nrow, 3 * LANES), BF16),
                        pltpu.VMEM((2, NT, SUBLANES, LANES), F32)],
        compiler_params=_cparams(3),
        name="rwkv_scan",
    )(kk, dec, b, kd, r, v, e, s0)


def _gate_kernel(yf_ref, yb_ref, bonus_ref, g_ref, lw_ref, lb_ref, o_ref, *, H, n):
    tt, _, DR = yf_ref.shape
    lnw = lw_ref[...]
    lnb = lb_ref[...]

    def groupnorm(y):
        yc = y - _head_sum(y, H) * (1.0 / n)
        var = _head_sum(yc * yc, H) * (1.0 / n)
        return yc * lax.rsqrt(var + LNX_EPS) * lnw + lnb

    flat = lambda ref: ref[...].reshape(tt * SUBLANES, DR)
    o = (groupnorm(flat(yf_ref)) + groupnorm(flat(yb_ref)) + flat(bonus_ref)) * flat(g_ref)
    o_ref[...] = jnp.swapaxes(o.reshape(tt, SUBLANES, DR), 0, 1).astype(o_ref.dtype)


def _rwkv_gate(y_dirs, bonus, g, lnw, lnb, G, L, H, n):
    DR = g.shape[1]
    tt = _tile(L, 32, 16)
    nt = L // tt
    t3 = lambda a: a.reshape(a.shape[:-2] + (G * L, SUBLANES, DR))
    dspec = lambda d: pl.BlockSpec((None, tt, SUBLANES, DR), lambda gi, i: (d, gi * nt + i, 0, 0))
    mspec = pl.BlockSpec((tt, SUBLANES, DR), lambda gi, i: (gi * nt + i, 0, 0))
    wspec = pl.BlockSpec((1, DR), lambda gi, i: (0, 0))
    out = pl.pallas_call(
        functools.partial(_gate_kernel, H=H, n=n),
        grid=(G, nt),
        in_specs=[dspec(0), dspec(1), mspec, mspec, wspec, wspec],
        out_specs=pl.BlockSpec((SUBLANES, tt, DR), lambda gi, i: (gi, i, 0)),
        out_shape=jax.ShapeDtypeStruct((G * SUBLANES, L, DR), BF16),
        compiler_params=_cparams(2),
        name="rwkv_gate",
    )(t3(y_dirs), t3(y_dirs), t3(bonus), t3(g), lnw.reshape(1, DR), lnb.reshape(1, DR))
    return out.reshape(G * SUBLANES * L, DR)


def _mm_tmajor_kernel(x_ref, w_ref, o_ref, *, tt):
    x = x_ref[...]
    y = jnp.dot(x.reshape(SUBLANES * tt, x.shape[-1]), w_ref[...], preferred_element_type=F32)
    o_ref[...] = jnp.swapaxes(y.reshape(SUBLANES, tt, y.shape[-1]), 0, 1)


def _matmul_time_major(x, w, B, L):
    K = x.shape[1]
    N = w.shape[1]
    G = B // SUBLANES
    tt = _tile(L, 128, 16)
    tn = _tile(N, 512)
    nt = L // tt
    return pl.pallas_call(
        functools.partial(_mm_tmajor_kernel, tt=tt),
        grid=(G, nt, N // tn),
        in_specs=[pl.BlockSpec((SUBLANES, tt, K), lambda g, i, j: (g, i, 0)),
                  pl.BlockSpec((K, tn), lambda g, i, j: (0, j))],
        out_specs=pl.BlockSpec((tt, SUBLANES, tn), lambda g, i, j: (g * nt + i, 0, j)),
        out_shape=jax.ShapeDtypeStruct((G * L, SUBLANES, N), F32),
        compiler_params=_cparams(3),
        name="proj_rwkv",
    )(x.reshape(B, L, K), w).reshape(B * L, N)


def _head_minor(a, H, n):
    return a.reshape(a.shape[:-1] + (H, n)).swapaxes(-1, -2).reshape(a.shape)


def _rwkv_branch(h, w_r, B, L, H, n, s_f0, s_b0, rw):
    assert B % SUBLANES == 0 and LANES % (2 * H) == 0 and n % (LANES // H) == 0, (B, H, n)
    DR = H * n
    G = B // SUBLANES
    VL = LANES // H
    NT = n // VL
    pr = _matmul_time_major(h, w_r, B, L)
    r, v, kk, dec, b, kd, g, bonus = _rwkv_pre(pr, L, rw, DR, H)
    s0 = jnp.stack([s_f0, s_b0]).astype(F32).reshape(2, G, SUBLANES, H, NT, VL, n)
    VB = _scan_value_tiles(NT)
    s0 = jnp.transpose(s0, (0, 1, 4, 6, 2, 5, 3)).reshape(2, G, NT // VB, VB, n, SUBLANES, LANES)
    s0 = jnp.swapaxes(s0, 3, 4)
    y_dirs, sT = _rwkv_scan(kk, dec, b, kd, r, v, s0, G, L, H, n)
    o_rwkv = _rwkv_gate(y_dirs, bonus, g, rw["lnx_w"], rw["lnx_b"], G, L, H, n)
    sT = jnp.swapaxes(sT, 3, 4).reshape(2, G, NT, n, SUBLANES, VL, H)
    sT = jnp.transpose(sT, (0, 1, 4, 6, 2, 5, 3)).reshape(2, B, H, n, n)
    return o_rwkv, sT[0], sT[1]


def _pad_rows(w, rows):
    return jnp.pad(w, [(0, 0)] * (w.ndim - 2) + [(0, rows - w.shape[-2]), (0, 0)])


def kernel(x_prompt, x_sample, cache_attn_k, cache_attn_v, state_rwkv_fwd, state_rwkv_bwd, c, c_ctx,
           w_mod, b_mod, norm1_w, norm2_w, norm_f_w, w_in, tshift_mu, attn_rpb,
           rwkv_w0, rwkv_w2, rwkv_a0, rwkv_a2, rwkv_g2, rwkv_k_k, rwkv_k_a, rwkv_r_k, lnx_w, lnx_b,
           w_o_attn, w_o_rwkv, w_out, w_mlp1, w_mlp2):
    Bc, Lc, D = x_prompt.shape
    Bs, Ls, _ = x_sample.shape
    depth = w_mod.shape[0]
    HA, dh = cache_attn_k.shape[3], cache_attn_k.shape[4]
    HR, n = state_rwkv_fwd.shape[2], state_rwkv_fwd.shape[3]
    DA, DR = HA * dh, HR * n
    lw, la, lg = rwkv_w2.shape[2], rwkv_a2.shape[2], rwkv_g2.shape[1]
    LW, LA, LG = _round_up(lw, LANES), _round_up(la, LANES), _round_up(lg, LANES)
    past = cache_attn_k.shape[2]
    rows = Ls // GRID_W

    xc = x_prompt.reshape(Bc * Lc, D)
    xs = x_sample.reshape(Bs * Ls, D)
    cond_rows = _round_up(Bs + 1, SUBLANES)
    cond = jnp.zeros((cond_rows, D), F32).at[:Bs].set(c).at[Bs].set(c_ctx)
    pad_c = lambda a, wdt: jnp.pad(a, [(0, 0)] * (a.ndim - 1) + [(0, wdt - a.shape[-1])])

    new_k, new_v, new_sf, new_sb = [], [], [], []
    for l in range(depth):
        mod3 = _adaln(cond, w_mod[l], b_mod[l]).reshape(cond_rows * 6, 1, D)

        wi = w_in[l]
        o1, o2 = 3 * DA, 3 * DA + 3 * DR
        w_a = wi[:, :o1].astype(BF16)
        hm = functools.partial(_head_minor, H=HR, n=n)
        rkv = wi[:, o1:o2].reshape(D, 3, DR)
        w_r = jnp.concatenate([hm(rkv).reshape(D, 3 * DR), pad_c(wi[:, o2:o2 + lw], LW),
                               pad_c(wi[:, o2 + lw:o2 + lw + la], LA),
                               pad_c(wi[:, o2 + lw + la:o2 + lw + la + lg], LG)], axis=1).astype(BF16)
        w_g = wi[:, o2 + lw + la + lg:].astype(BF16)
        mu = tshift_mu[l]
        mu_r = jnp.concatenate([hm(mu[:3 * DR].reshape(3, DR)).reshape(-1), pad_c(mu[3 * DR:3 * DR + lw], LW),
                                pad_c(mu[3 * DR + lw:3 * DR + lw + la], LA),
                                pad_c(mu[3 * DR + lw + la:], LG)]).reshape(1, -1)
        rw = dict(mu=mu_r, LW=LW, LA=LA,
                  w0=hm(rwkv_w0[l]).reshape(2, 1, DR), w2=_pad_rows(hm(rwkv_w2[l]), LW).astype(BF16),
                  a0=hm(rwkv_a0[l]).reshape(2, 1, DR), a2=_pad_rows(hm(rwkv_a2[l]), LA).astype(BF16),
                  g2=_pad_rows(hm(rwkv_g2[l]), LG).astype(BF16),
                  k_k=hm(rwkv_k_k[l]), k_a=hm(rwkv_k_a[l]), r_k=hm(rwkv_r_k[l].reshape(-1)),
                  lnx_w=hm(lnx_w[l]), lnx_b=hm(lnx_b[l]))
        w_oa = w_o_attn[l].astype(BF16)
        w_or = hm(w_o_rwkv[l].T).T.astype(BF16)
        w_o, w_1, w_2 = w_out[l].astype(BF16), w_mlp1[l].astype(BF16), w_mlp2[l].astype(BF16)
        bias_tab = _nbr_bias_table(attn_rpb[l], rows)

        def block(x, B, L, rows_per_cond, cond_base, attention, s_f0, s_b0, pa_dtype):
            h = _norm_modulate(x, norm1_w[l], mod3, (0, 1), rows_per_cond, cond_base)
            pa = _matmul(h, w_a, pa_dtype, name="proj_attn")
            pg = _matmul(h, w_g, F32, name="proj_gate")
            o_attn = attention(pa)
            o_rwkv, s_f, s_b = _rwkv_branch(h, w_r, B, L, HR, n, s_f0, s_b0, rw)
            merged = _merge(o_attn, w_oa, o_rwkv, w_or, pg)
            x = _matmul_gated_residual(merged, w_o, x, mod3, 2, rows_per_cond, cond_base, "out_proj")
            h = _norm_modulate(x, norm2_w[l], mod3, (3, 4), rows_per_cond, cond_base)
            a = _matmul(h, w_1, BF16, relu2=True, name="mlp_up")
            x = _matmul_gated_residual(a, w_2, x, mod3, 5, rows_per_cond, cond_base, "mlp_down")
            return x, pa, s_f, s_b

        z = jnp.zeros((Bc, HR, n, n), F32)
        xc, pa_c, s_f, s_b = block(
            xc, Bc, Lc, Bc * Lc, Bs,
            lambda pa: _context_attention(pa, Bc, Lc, HA, dh), z, z, F32)
        new_k.append(pa_c[:, DA:2 * DA].reshape(Bc, Lc, HA, dh))
        new_v.append(pa_c[:, 2 * DA:].reshape(Bc, Lc, HA, dh))
        new_sf.append(s_f)
        new_sb.append(s_b)

        kc = cache_attn_k[:, l].reshape(Bs * past, DA)
        vc = cache_attn_v[:, l].reshape(Bs * past, DA)
        xs, _, _, _ = block(
            xs, Bs, Ls, Ls, 0,
            lambda pa: _neighbourhood_attention(pa, kc, vc, bias_tab, Bs, Ls, HA, dh),
            state_rwkv_fwd[:, l], state_rwkv_bwd[:, l], BF16)

    y_prompt = _final_norm(xc, norm_f_w).reshape(Bc, Lc, D)
    y_sample = _final_norm(xs, norm_f_w).reshape(Bs, Ls, D)
    return (y_prompt, y_sample, jnp.stack(new_k, axis=1), jnp.stack(new_v, axis=1),
            jnp.stack(new_sf, axis=1), jnp.stack(new_sb, axis=1))
```
